```python
import math
import jax
import jax.numpy as jnp
from jax import lax
import numpy as np

D_MODEL = 2048
BATCH = 1
SEQ = 8192
DEPTH = 1
DEC_BATCH = 32
DEC_SEQ = 1
PAST_LEN = 8192
PAGE_SIZE = 128

N_META = 16
BLOCK = 128
H_FOX = 8
D_FOX = 128
H_DIFF = 8
D_DIFF = 64
DV_DIFF = 128
MIX_WIDTH = H_FOX * D_FOX + H_DIFF * DV_DIFF
D_FF = 4 * D_MODEL
ROPE_THETA = 500000.0
ROT_DIM = D_DIFF // 4
ALPHA = (2 * DEPTH) ** 0.25
BETA = (8 * DEPTH) ** -0.25
LN_EPS = 1e-5
FORGET_BIAS_INIT = 4.0
NEG_INF = -1e30
COL_SIZES = (H_FOX * D_FOX, H_FOX * D_FOX, H_FOX * D_FOX, H_FOX,
             H_DIFF * 2 * D_DIFF, H_DIFF * 2 * D_DIFF, H_DIFF * DV_DIFF)
D_IN = sum(COL_SIZES)
COL_OFFSETS = tuple(sum(COL_SIZES[:i + 1]) for i in range(len(COL_SIZES) - 1))

kernel_name = 'fox_diff_hymba_decoder_step'


def layer_norm(x, g, b):
    xf = x.astype(jnp.float32)
    mu = jnp.mean(xf, -1, keepdims=True)
    var = jnp.mean(jnp.square(xf - mu), -1, keepdims=True)
    y = (xf - mu) * lax.rsqrt(var + LN_EPS) * g.astype(jnp.float32) + b.astype(jnp.float32)
    return y.astype(x.dtype)


def rope_partial(x, pos):
    half = ROT_DIM // 2
    inv_freq = jnp.power(jnp.float32(ROPE_THETA), -jnp.arange(0, ROT_DIM, 2, dtype=jnp.float32) / ROT_DIM)
    ang = pos.astype(jnp.float32)[:, None] * inv_freq[None, :]
    cos = jnp.cos(ang)[None, :, None, None, :]
    sin = jnp.sin(ang)[None, :, None, None, :]
    xr = x[..., :ROT_DIM].astype(jnp.float32)
    x1, x2 = xr[..., :half], xr[..., half:]
    rot = jnp.concatenate([x1 * cos - x2 * sin, x2 * cos + x1 * sin], -1)
    return jnp.concatenate([rot.astype(x.dtype), x[..., ROT_DIM:]], -1)


def project(h, w_in, b_forget, pos):
    B, T, _ = h.shape
    z = jnp.einsum('btd,de->bte', h, w_in)
    fq, fk, fv, ff, dq, dk, dv = jnp.split(z, COL_OFFSETS, axis=-1)
    fq = fq.reshape(B, T, H_FOX, D_FOX)
    fk = fk.reshape(B, T, H_FOX, D_FOX)
    fv = fv.reshape(B, T, H_FOX, D_FOX)
    logf = jax.nn.log_sigmoid(ff.astype(jnp.float32) + b_forget.astype(jnp.float32))
    dq = rope_partial(dq.reshape(B, T, H_DIFF, 2, D_DIFF), pos).reshape(B, T, H_DIFF, 2 * D_DIFF)
    dk = rope_partial(dk.reshape(B, T, H_DIFF, 2, D_DIFF), pos).reshape(B, T, H_DIFF, 2 * D_DIFF)
    dv = dv.reshape(B, T, H_DIFF, DV_DIFF)
    return fq, fk, fv, logf, dq, dk, dv


def fox_core(q, k, v, cq, ck, mask):
    s = jnp.einsum('bthd,bshd->bhts', q, k, preferred_element_type=jnp.float32) * (D_FOX ** -0.5)
    s = s + (jnp.swapaxes(cq, 1, 2)[..., :, None] - jnp.swapaxes(ck, 1, 2)[..., None, :])
    p = jax.nn.softmax(jnp.where(mask, s, NEG_INF), axis=-1)
    return jnp.einsum('bhts,bshd->bthd', p.astype(v.dtype), v, preferred_element_type=jnp.float32)


def diff_core(q, k, v, lam, mask):
    scale = D_DIFF ** -0.5

    def attn_map(qh, kh):
        s = jnp.einsum('bthd,bshd->bhts', qh, kh, preferred_element_type=jnp.float32) * scale
        return jax.nn.softmax(jnp.where(mask, s, NEG_INF), axis=-1)

    p = attn_map(q[..., :D_DIFF], k[..., :D_DIFF]) - lam * attn_map(q[..., D_DIFF:], k[..., D_DIFF:])
    return jnp.einsum('bhts,bshd->bthd', p.astype(v.dtype), v, preferred_element_type=jnp.float32)


def prompt_attention(fq, fk, fv, logf, dq, dk, dv, lam):
    B, L = fq.shape[:2]
    pad = (-N_META) % BLOCK
    padf = lambda a: jnp.pad(a, [(0, 0), (pad, 0)] + [(0, 0)] * (a.ndim - 2))
    fq, fk, fv, logf, dq, dk, dv = [padf(a) for a in (fq, fk, fv, logf, dq, dk, dv)]
    Lp = L + pad
    n_blocks = Lp // BLOCK
    c = jnp.cumsum(logf, axis=1)
    kidx = jnp.arange(Lp)

    def one_block(i):
        start = i * BLOCK
        sl = lambda a: lax.dynamic_slice_in_dim(a, start, BLOCK, axis=1)
        qidx = start + jnp.arange(BLOCK)
        mask = (kidx[None, :] <= qidx[:, None]) & (kidx[None, :] >= pad)
        return (fox_core(sl(fq), fk, fv, sl(c), c, mask), diff_core(sl(dq), dk, dv, lam, mask))

    of, od = lax.map(one_block, jnp.arange(n_blocks))
    unblock = lambda o: jnp.moveaxis(o, 0, 1).reshape((B, Lp) + o.shape[3:])[:, pad:]
    return unblock(of), unblock(od)


def sample_attention(l, fq, fk, fv, logf, dq, dk, dv, lam, cache_fox_k, cache_fox_v,
                     cache_fox_logf, cache_diff_k, cache_diff_v, page_table):
    T = fq.shape[1]
    S = PAST_LEN + T
    kidx = jnp.arange(S)
    qidx = PAST_LEN + jnp.arange(T)
    mask = kidx[None, :] <= qidx[:, None]

    def past(cache, pages):
        rows = cache[l, pages]
        return rows.reshape((PAST_LEN,) + cache.shape[3:])

    def one_seq(args):
        pages, fq_b, fk_b, fv_b, lf_b, dq_b, dk_b, dv_b = args
        cat = lambda cache, new: jnp.concatenate([past(cache, pages), new.astype(cache.dtype)], 0)[None]
        c = jnp.cumsum(cat(cache_fox_logf, lf_b).astype(jnp.float32), axis=1)
        of = fox_core(fq_b[None], cat(cache_fox_k, fk_b), cat(cache_fox_v, fv_b), c[:, PAST_LEN:], c, mask)[0]
        od = diff_core(dq_b[None], cat(cache_diff_k, dk_b), cat(cache_diff_v, dv_b), lam, mask)[0]
        return of, od

    return lax.map(one_seq, (page_table, fq, fk, fv, logf, dq, dk, dv))


def mix_out_and_ffn(x, o_fox, o_diff, diff_norm_g, lam_init, w_o, ln1_g, ln1_b, w_up, w_down, ln2_g, ln2_b):
    B, T = x.shape[:2]
    od = o_diff.astype(jnp.float32)
    od = od * lax.rsqrt(jnp.mean(jnp.square(od), -1, keepdims=True) + LN_EPS)
    od = od * diff_norm_g.astype(jnp.float32) * (1.0 - lam_init)
    o = jnp.concatenate([o_fox.reshape(B, T, -1), od.reshape(B, T, -1)], -1).astype(x.dtype)
    x = layer_norm(ALPHA * x + jnp.einsum('btm,md->btd', o, w_o), ln1_g, ln1_b)
    u = jnp.square(jax.nn.relu(jnp.einsum('btd,df->btf', x, w_up)))
    x = layer_norm(ALPHA * x + jnp.einsum('btf,fd->btd', u, w_down), ln2_g, ln2_b)
    return x


def setup_inputs(seed: int = 0) -> dict:
    key = jax.random.key(seed)
    ks = jax.random.split(key, 26)
    f32 = jnp.float32
    n_pages = PAST_LEN // PAGE_SIZE
    n_used = DEC_BATCH * n_pages
    n_pool = n_used + max(1, n_used // 4)
    nrm = lambda k, shape, s=1.0: jax.random.normal(k, shape, f32) * s
    x_prompt = nrm(ks[0], (BATCH, SEQ, D_MODEL))
    x_sample = nrm(ks[1], (DEC_BATCH, DEC_SEQ, D_MODEL))
    cache_fox_k = nrm(ks[2], (DEPTH, n_pool, PAGE_SIZE, H_FOX, D_FOX))
    cache_fox_v = nrm(ks[3], (DEPTH, n_pool, PAGE_SIZE, H_FOX, D_FOX))
    cache_fox_logf = jax.nn.log_sigmoid(FORGET_BIAS_INIT + nrm(ks[4], (DEPTH, n_pool, PAGE_SIZE, H_FOX)))
    cache_diff_k = nrm(ks[5], (DEPTH, n_pool, PAGE_SIZE, H_DIFF, 2 * D_DIFF))
    cache_diff_v = nrm(ks[6], (DEPTH, n_pool, PAGE_SIZE, H_DIFF, DV_DIFF))
    page_table = jax.random.permutation(ks[7], n_pool)[:n_used].reshape(DEC_BATCH, n_pages).astype(jnp.int32)
    meta_tokens = nrm(ks[8], (N_META, D_MODEL))
    ln_in_g = 1.0 + nrm(ks[9], (D_MODEL,), 0.02)
    ln_in_b = nrm(ks[10], (D_MODEL,), 0.02)
    col_scale = jnp.concatenate([jnp.full((n,), BETA if i in (2, 6) else 1.0, f32)
                                 for i, n in enumerate(COL_SIZES)])
    w_in = nrm(ks[11], (DEPTH, D_MODEL, D_IN), D_MODEL ** -0.5) * col_scale
    b_forget = FORGET_BIAS_INIT + nrm(ks[12], (DEPTH, H_FOX), 0.1)
    lambda_q1 = nrm(ks[13], (DEPTH, D_DIFF), 0.1)
    lambda_k1 = nrm(ks[14], (DEPTH, D_DIFF), 0.1)
    lambda_q2 = nrm(ks[15], (DEPTH, D_DIFF), 0.1)
    lambda_k2 = nrm(ks[16], (DEPTH, D_DIFF), 0.1)
    diff_norm_g = 1.0 + nrm(ks[17], (DEPTH, DV_DIFF), 0.02)
    w_o = nrm(ks[18], (DEPTH, MIX_WIDTH, D_MODEL), MIX_WIDTH ** -0.5 * BETA)
    ln1_g = 1.0 + nrm(ks[19], (DEPTH, D_MODEL), 0.02)
    ln1_b = nrm(ks[20], (DEPTH, D_MODEL), 0.02)
    w_up = nrm(ks[21], (DEPTH, D_MODEL, D_FF), D_MODEL ** -0.5 * BETA)
    w_down = nrm(ks[22], (DEPTH, D_FF, D_MODEL), D_FF ** -0.5 * BETA)
    ln2_g = 1.0 + nrm(ks[23], (DEPTH, D_MODEL), 0.02)
    ln2_b = nrm(ks[24], (DEPTH, D_MODEL), 0.02)
    return {'x_prompt': x_prompt, 'x_sample': x_sample,
            'cache_fox_k': cache_fox_k, 'cache_fox_v': cache_fox_v, 'cache_fox_logf': cache_fox_logf,
            'cache_diff_k': cache_diff_k, 'cache_diff_v': cache_diff_v, 'page_table': page_table,
            'meta_tokens': meta_tokens, 'ln_in_g': ln_in_g, 'ln_in_b': ln_in_b,
            'w_in': w_in, 'b_forget': b_forget,
            'lambda_q1': lambda_q1, 'lambda_k1': lambda_k1, 'lambda_q2': lambda_q2, 'lambda_k2': lambda_k2,
            'diff_norm_g': diff_norm_g, 'w_o': w_o, 'ln1_g': ln1_g, 'ln1_b': ln1_b,
            'w_up': w_up, 'w_down': w_down, 'ln2_g': ln2_g, 'ln2_b': ln2_b}


def reference(x_prompt, x_sample, cache_fox_k, cache_fox_v, cache_fox_logf, cache_diff_k, cache_diff_v,
              page_table, meta_tokens, ln_in_g, ln_in_b, w_in, b_forget,
              lambda_q1, lambda_k1, lambda_q2, lambda_k2, diff_norm_g, w_o, ln1_g, ln1_b,
              w_up, w_down, ln2_g, ln2_b):
    B = x_prompt.shape[0]
    T_s = x_sample.shape[1]
    meta = jnp.broadcast_to(meta_tokens[None].astype(x_prompt.dtype), (B, N_META, D_MODEL))
    hp = layer_norm(jnp.concatenate([meta, x_prompt], 1), ln_in_g, ln_in_b)
    hs = layer_norm(x_sample, ln_in_g, ln_in_b)
    pos_p = jnp.arange(hp.shape[1])
    pos_s = PAST_LEN + jnp.arange(T_s)
    p_rows = [[], [], [], [], []]
    s_rows = [[], [], [], [], []]
    for l in range(DEPTH):
        lam_init = 0.8 - 0.6 * math.exp(-0.3 * l)
        f32 = jnp.float32
        lam = (jnp.exp(jnp.sum(lambda_q1[l].astype(f32) * lambda_k1[l].astype(f32)))
               - jnp.exp(jnp.sum(lambda_q2[l].astype(f32) * lambda_k2[l].astype(f32))) + lam_init)
        post = lambda h, of, od: mix_out_and_ffn(h, of, od, diff_norm_g[l], lam_init, w_o[l], ln1_g[l], ln1_b[l],
                                                 w_up[l], w_down[l], ln2_g[l], ln2_b[l])
        fq, fk, fv, logf, dq, dk, dv = project(hp, w_in[l], b_forget[l], pos_p)
        of, od = prompt_attention(fq, fk, fv, logf, dq, dk, dv, lam)
        hp = post(hp, of, od)
        for lst, a in zip(p_rows, (fk, fv, logf, dk, dv)):
            lst.append(a)
        fq, fk, fv, logf, dq, dk, dv = project(hs, w_in[l], b_forget[l], pos_s)
        of, od = sample_attention(l, fq, fk, fv, logf, dq, dk, dv, lam, cache_fox_k, cache_fox_v,
                                  cache_fox_logf, cache_diff_k, cache_diff_v, page_table)
        hs = post(hs, of, od)
        for lst, a in zip(s_rows, (fk, fv, logf, dk, dv)):
            lst.append(a)
    y_prompt = hp[:, N_META:]
    y_sample = hs
    p_fox_k, p_fox_v, p_fox_logf, p_diff_k, p_diff_v = [jnp.stack(r, 0) for r in p_rows]
    s_fox_k, s_fox_v, s_fox_logf, s_diff_k, s_diff_v = [jnp.stack(r, 0) for r in s_rows]
    return (y_prompt, y_sample, p_fox_k, p_fox_v, p_fox_logf, p_diff_k, p_diff_v,
            s_fox_k, s_fox_v, s_fox_logf, s_diff_k, s_diff_v)
```

```python
import functools
import math

import jax
import jax.numpy as jnp
from jax import lax
from jax.experimental import pallas as pl
from jax.experimental.pallas import tpu as pltpu

F32 = jnp.float32
BF16 = jnp.bfloat16

N_META = 16
N_HEADS = 8
D_HEAD = 128
D_MAP = 64
ROT_DIM = D_MAP // 4
ROPE_THETA = 500000.0
DEPTH = 1
ALPHA = (2 * DEPTH) ** 0.25
LN_EPS = 1e-5
NEG_INF = -1e30
LAM_INIT = 0.8 - 0.6 * math.exp(-0.3 * 0)
FOX_SCALE = D_HEAD ** -0.5
DIFF_SCALE = D_MAP ** -0.5
GROUP = N_HEADS * D_HEAD
BIAS_LANES = D_HEAD // N_HEADS

LANE = 128
VMEM_LIMIT = 56 * 1024 * 1024

NT_DIMS = (((1,), (1,)), ((), ()))


def _round_up(x, m):
    return (x + m - 1) // m * m


def _pick_tile(n, options):
    for t in options:
        if n % t == 0:
            return t
    raise ValueError(f"no tile in {options} divides {n}")


def _layer_norm(x, g, b):
    mu = jnp.mean(x, -1, keepdims=True)
    xc = x - mu
    var = jnp.mean(xc * xc, -1, keepdims=True)
    return xc * lax.rsqrt(var + LN_EPS) * g + b


def _split3(x):
    hi = x.astype(BF16)
    r1 = x - hi.astype(F32)
    mid = r1.astype(BF16)
    lo = (r1 - mid.astype(F32)).astype(BF16)
    return hi, mid, lo


def _rope_store(z, cos, sa, sb, scale, out_refs):
    for h in range(N_HEADS):
        zh = z[:, h * D_HEAD:(h + 1) * D_HEAD]
        yh = zh * cos + pltpu.roll(zh, D_HEAD - ROT_DIM // 2, 1) * sa + pltpu.roll(zh, ROT_DIM // 2, 1) * sb
        if scale != 1.0:
            yh = yh * scale
        for ref in out_refs:
            ref[:, h * D_HEAD:(h + 1) * D_HEAD] = yh.astype(ref.dtype)


def _proj_kernel(x_ref, g_ref, b_ref, wm_ref, wf_ref, bf_ref, cos_ref, sa_ref, sb_ref,
                 h_ref, zf_ref, zb_ref, qb_ref, kb_ref, lf_ref, hb_scr, carry_scr, *, tm):
    i = pl.program_id(0)
    j = pl.program_id(1)

    @pl.when(j == 0)
    def _():
        hf = _layer_norm(x_ref[...], g_ref[...], b_ref[...])
        h_ref[...] = hf
        hb = hf.astype(BF16)
        hb_scr[...] = hb
        ff = jnp.dot(hb, wf_ref[...], preferred_element_type=F32) + bf_ref[...]
        lf = jnp.minimum(ff, 0.0) - jnp.log1p(jnp.exp(-jnp.abs(ff)))
        lf_ref[...] = lf

        @pl.when(i == 0)
        def _():
            carry_scr[...] = jnp.zeros_like(carry_scr)

        row = lax.broadcasted_iota(jnp.int32, (tm, tm), 0)
        col = lax.broadcasted_iota(jnp.int32, (tm, tm), 1)
        tri = (col <= row).astype(F32)
        cs = jnp.dot(tri, lf, precision=lax.Precision.HIGHEST, preferred_element_type=F32) + carry_scr[0:1, :]
        carry_scr[...] = jnp.broadcast_to(cs[tm - 1:tm, :], carry_scr.shape)
        hi, mid, lo = [p.astype(F32) for p in _split3(cs)]
        lane = lax.broadcasted_iota(jnp.int32, (tm, LANE), 1) & (BIAS_LANES - 1)
        one = jnp.ones((tm, LANE), F32)
        zero = jnp.zeros((tm, LANE), F32)
        qb_ref[...] = jnp.where(lane == 0, hi, jnp.where(lane == 1, mid, jnp.where(
            lane == 2, lo, jnp.where(lane < 6, one, zero)))).astype(BF16)
        kb_ref[...] = jnp.where(lane < 3, one, jnp.where(lane == 3, -hi, jnp.where(
            lane == 4, -mid, jnp.where(lane == 5, -lo, zero)))).astype(BF16)

    z = jnp.dot(hb_scr[...], wm_ref[...], preferred_element_type=F32)

    @pl.when(j == 0)
    def _():
        zf_ref[...] = z
        zb_ref[...] = (z * FOX_SCALE).astype(BF16)

    @pl.when((j == 1) | (j == 2) | (j == 5))
    def _():
        zf_ref[...] = z
        zb_ref[...] = z.astype(BF16)

    @pl.when(j == 3)
    def _():
        zf_ref[...] = z
        _rope_store(z, cos_ref[...], sa_ref[...], sb_ref[...], DIFF_SCALE, (zb_ref,))

    @pl.when(j == 4)
    def _():
        _rope_store(z, cos_ref[...], sa_ref[...], sb_ref[...], 1.0, (zf_ref, zb_ref))


def _project(x_all, ln_g, ln_b, w_main, w_f, b_f, cos_t, sa_t, sb_t, tm):
    R, D = x_all.shape
    n_groups = w_main.shape[1] // GROUP
    row = lambda i, j: (i, 0)
    const = lambda i, j: (0, 0)
    return pl.pallas_call(
        functools.partial(_proj_kernel, tm=tm),
        grid=(R // tm, n_groups),
        in_specs=[
            pl.BlockSpec((tm, D), row),
            pl.BlockSpec((1, D), const),
            pl.BlockSpec((1, D), const),
            pl.BlockSpec((D, GROUP), lambda i, j: (0, j)),
            pl.BlockSpec((D, LANE), const),
            pl.BlockSpec((1, LANE), const),
            pl.BlockSpec((tm, LANE), row),
            pl.BlockSpec((tm, LANE), row),
            pl.BlockSpec((tm, LANE), row),
        ],
        out_specs=[
            pl.BlockSpec((tm, D), row),
            pl.BlockSpec((tm, GROUP), lambda i, j: (i, j)),
            pl.BlockSpec((tm, GROUP), lambda i, j: (i, j)),
            pl.BlockSpec((tm, LANE), row),
            pl.BlockSpec((tm, LANE), row),
            pl.BlockSpec((tm, LANE), row),
        ],
        out_shape=[
            jax.ShapeDtypeStruct((R, D), F32),
            jax.ShapeDtypeStruct((R, n_groups * GROUP), F32),
            jax.ShapeDtypeStruct((R, n_groups * GROUP), BF16),
            jax.ShapeDtypeStruct((R, LANE), BF16),
            jax.ShapeDtypeStruct((R, LANE), BF16),
            jax.ShapeDtypeStruct((R, LANE), F32),
        ],
        scratch_shapes=[pltpu.VMEM((tm, D), BF16), pltpu.VMEM((8, LANE), F32)],
        compiler_params=pltpu.CompilerParams(
            dimension_semantics=("arbitrary", "arbitrary"), vmem_limit_bytes=VMEM_LIMIT),
        name="ln_proj",
    )(x_all, ln_g, ln_b, w_main, w_f, b_f, cos_t, sa_t, sb_t)


def _flash_tile(qs_scr, k_ref, kb, v_ref, m_scr, acc_scr, *, tq, tk, nmap, masked):
    M = nmap * tq
    ones = jnp.ones((tk, D_HEAD), BF16)
    if masked:
        row = lax.broadcasted_iota(jnp.int32, (M, tk), 0)
        col = lax.broadcasted_iota(jnp.int32, (M, tk), 1)
        if nmap == 2:
            row = jnp.where(row >= tq, row - tq, row)
        keep = col <= row
    for h in range(N_HEADS):
        kh = k_ref[:, h * D_HEAD:(h + 1) * D_HEAD]
        if kb is not None:
            kh = jnp.concatenate([kh, kb], axis=1)
        s = lax.dot_general(qs_scr[h], kh, NT_DIMS, preferred_element_type=F32)
        if masked:
            s = jnp.where(keep, s, NEG_INF)
        m_prev = m_scr[h]
        m_new = jnp.maximum(m_prev, jnp.max(s, axis=1, keepdims=True))
        p = jnp.exp(s - jnp.concatenate([m_new] * (tk // LANE), axis=1))
        alpha = jnp.exp(m_prev - m_new)
        vh = jnp.concatenate([v_ref[:, h * D_HEAD:(h + 1) * D_HEAD], ones], axis=1)
        pv = jnp.dot(p.astype(BF16), vh, preferred_element_type=F32)
        acc_scr[h] = acc_scr[h] * jnp.concatenate([alpha, alpha], axis=1) + pv
        m_scr[h] = m_new


def _flash_init(m_scr, acc_scr):
    m_scr[...] = jnp.full(m_scr.shape, NEG_INF, F32)
    acc_scr[...] = jnp.zeros(acc_scr.shape, F32)


def _fox_kernel(qi_tab, ki_tab, q_ref, k_ref, v_ref, qb_ref, kb_ref, o_ref, qs_scr, m_scr, acc_scr, *, tq):
    s = pl.program_id(0)
    qi = qi_tab[s]
    ki = ki_tab[s]

    @pl.when(ki == 0)
    def _():
        _flash_init(m_scr, acc_scr)
        qb = qb_ref[...].astype(F32)
        head_of_lane = lax.broadcasted_iota(jnp.int32, qb.shape, 1) // BIAS_LANES
        for h in range(N_HEADS):
            qbh = jnp.where(head_of_lane == h, qb, 0.0).astype(BF16)
            qs_scr[h] = jnp.concatenate([q_ref[:, h * D_HEAD:(h + 1) * D_HEAD], qbh], axis=1)

    tile = functools.partial(_flash_tile, qs_scr, k_ref, kb_ref[...], v_ref, m_scr, acc_scr,
                             tq=tq, tk=tq, nmap=1)

    @pl.when(ki != qi)
    def _():
        tile(masked=False)

    @pl.when(ki == qi)
    def _():
        tile(masked=True)
        for h in range(N_HEADS):
            a = acc_scr[h]
            o_ref[:, h * D_HEAD:(h + 1) * D_HEAD] = (a[:, :D_HEAD] / a[:, D_HEAD:]).astype(o_ref.dtype)


def _diff_lambda(lamp_ref):
    lp = lamp_ref[...]
    return (jnp.exp(jnp.sum(lp[0:1] * lp[1:2], axis=1, keepdims=True))
            - jnp.exp(jnp.sum(lp[2:3] * lp[3:4], axis=1, keepdims=True)) + LAM_INIT)


def _diff_finish(o1, o2, lam, gn):
    od = o1 - lam * o2
    od = od * lax.rsqrt(jnp.mean(od * od, axis=-1, keepdims=True) + LN_EPS)
    return od * gn * (1.0 - LAM_INIT)


def _diff_kernel(qi_tab, ki_tab, q_ref, k_ref, v_ref, lamp_ref, gn_ref, o_ref, qs_scr, m_scr, acc_scr, *, tq):
    s = pl.program_id(0)
    qi = qi_tab[s]
    ki = ki_tab[s]

    @pl.when(ki == 0)
    def _():
        _flash_init(m_scr, acc_scr)
        lane = lax.broadcasted_iota(jnp.int32, (tq, D_HEAD), 1)
        for h in range(N_HEADS):
            qh = q_ref[:, h * D_HEAD:(h + 1) * D_HEAD].astype(F32)
            qs_scr[h, 0:tq, :] = jnp.where(lane < D_MAP, qh, 0.0).astype(BF16)
            qs_scr[h, tq:2 * tq, :] = jnp.where(lane >= D_MAP, qh, 0.0).astype(BF16)

    tile = functools.partial(_flash_tile, qs_scr, k_ref, None, v_ref, m_scr, acc_scr, tq=tq, tk=tq, nmap=2)

    @pl.when(ki != qi)
    def _():
        tile(masked=False)

    @pl.when(ki == qi)
    def _():
        tile(masked=True)
        lam = _diff_lambda(lamp_ref)
        gn = gn_ref[...]
        for h in range(N_HEADS):
            a = acc_scr[h]
            o1 = a[0:tq, :D_HEAD] / a[0:tq, D_HEAD:]
            o2 = a[tq:2 * tq, :D_HEAD] / a[tq:2 * tq, D_HEAD:]
            o_ref[:, h * D_HEAD:(h + 1) * D_HEAD] = _diff_finish(o1, o2, lam, gn).astype(o_ref.dtype)


def _causal_steps(n_tiles):
    qi = [q for q in range(n_tiles) for _ in range(q + 1)]
    ki = [k for q in range(n_tiles) for k in range(q + 1)]
    return jnp.asarray(qi, jnp.int32), jnp.asarray(ki, jnp.int32)


def _fox_attention(zb, qb, kb, tq):
    R = zb.shape[0]
    qi_tab, ki_tab = _causal_steps(R // tq)
    grid_spec = pltpu.PrefetchScalarGridSpec(
        num_scalar_prefetch=2,
        grid=(qi_tab.shape[0],),
        in_specs=[
            pl.BlockSpec((tq, GROUP), lambda s, qt, kt: (qt[s], 0)),
            pl.BlockSpec((tq, GROUP), lambda s, qt, kt: (kt[s], 1)),
            pl.BlockSpec((tq, GROUP), lambda s, qt, kt: (kt[s], 2)),
            pl.BlockSpec((tq, LANE), lambda s, qt, kt: (qt[s], 0)),
            pl.BlockSpec((tq, LANE), lambda s, qt, kt: (kt[s], 0)),
        ],
        out_specs=pl.BlockSpec((tq, GROUP), lambda s, qt, kt: (qt[s], 0)),
        scratch_shapes=[
            pltpu.VMEM((N_HEADS, tq, 2 * D_HEAD), BF16),
            pltpu.VMEM((N_HEADS, tq, LANE), F32),
            pltpu.VMEM((N_HEADS, tq, 2 * D_HEAD), F32),
        ],
    )
    return pl.pallas_call(
        functools.partial(_fox_kernel, tq=tq),
        grid_spec=grid_spec,
        out_shape=jax.ShapeDtypeStruct((R, GROUP), BF16),
        compiler_params=pltpu.CompilerParams(dimension_semantics=("arbitrary",), vmem_limit_bytes=VMEM_LIMIT),
        name="fox_flash",
    )(qi_tab, ki_tab, zb, zb, zb, qb, kb)


def _diff_attention(zb, lamp, gn, tq):
    R = zb.shape[0]
    qi_tab, ki_tab = _causal_steps(R // tq)
    grid_spec = pltpu.PrefetchScalarGridSpec(
        num_scalar_prefetch=2,
        grid=(qi_tab.shape[0],),
        in_specs=[
            pl.BlockSpec((tq, GROUP), lambda s, qt, kt: (qt[s], 3)),
            pl.BlockSpec((tq, GROUP), lambda s, qt, kt: (kt[s], 4)),
            pl.BlockSpec((tq, GROUP), lambda s, qt, kt: (kt[s], 5)),
            pl.BlockSpec((8, LANE), lambda s, qt, kt: (0, 0)),
            pl.BlockSpec((1, LANE), lambda s, qt, kt: (0, 0)),
        ],
        out_specs=pl.BlockSpec((tq, GROUP), lambda s, qt, kt: (qt[s], 0)),
        scratch_shapes=[
            pltpu.VMEM((N_HEADS, 2 * tq, D_HEAD), BF16),
            pltpu.VMEM((N_HEADS, 2 * tq, LANE), F32),
            pltpu.VMEM((N_HEADS, 2 * tq, 2 * D_HEAD), F32),
        ],
    )
    return pl.pallas_call(
        functools.partial(_diff_kernel, tq=tq),
        grid_spec=grid_spec,
        out_shape=jax.ShapeDtypeStruct((R, GROUP), BF16),
        compiler_params=pltpu.CompilerParams(dimension_semantics=("arbitrary",), vmem_limit_bytes=VMEM_LIMIT),
        name="diff_flash",
    )(qi_tab, ki_tab, zb, zb, zb, lamp, gn)


def _rep(x, n):
    return jnp.concatenate([x] * n, axis=1)


def _decode_kernel(pt_ref, *refs, pages_per_step):
    P = pages_per_step
    (qf_ref, kf_ref, vf_ref, lfo_ref, qd_ref, kd_ref, vd_ref, ux_ref, lamp_ref, gn_ref) = refs[:10]
    kc = refs[10:10 + P]
    vc = refs[10 + P:10 + 2 * P]
    lc = refs[10 + 2 * P:10 + 3 * P]
    dkc = refs[10 + 3 * P:10 + 4 * P]
    dvc = refs[10 + 4 * P:10 + 5 * P]
    of_ref, od_ref = refs[10 + 5 * P:12 + 5 * P]
    mf, lf, af, cf, qf16, md, ld, ad, qd16 = refs[12 + 5 * P:]
    j = pl.program_id(1)
    page_rows = kc[0].shape[0]
    n_rep = page_rows // LANE
    dup = lambda x: jnp.concatenate([x, x], axis=0)

    @pl.when(j == 0)
    def _():
        q2 = dup(qf_ref[0])
        qf16[...] = q2.astype(BF16)
        s_own = jnp.sum(q2 * dup(kf_ref[0]), axis=1, keepdims=True)
        mf[...] = jnp.broadcast_to(s_own, mf.shape)
        lf[...] = jnp.ones(lf.shape, F32)
        af[...] = dup(vf_ref[0])
        cf[...] = dup(lfo_ref[0])
        lane = lax.broadcasted_iota(jnp.int32, (N_HEADS, D_HEAD), 1)
        qd = qd_ref[0]
        q16 = jnp.concatenate([jnp.where(lane < D_MAP, qd, 0.0), jnp.where(lane >= D_MAP, qd, 0.0)], axis=0)
        qd16[...] = q16.astype(BF16)
        sd_own = jnp.sum(q16 * dup(kd_ref[0]), axis=1, keepdims=True)
        md[...] = jnp.broadcast_to(sd_own, md.shape)
        ld[...] = jnp.ones(ld.shape, F32)
        ad[...] = dup(vd_ref[0])

    lane8 = lax.broadcasted_iota(jnp.int32, (2 * N_HEADS, page_rows), 1) & (N_HEADS - 1)
    row8 = lax.broadcasted_iota(jnp.int32, (2 * N_HEADS, page_rows), 0) & (N_HEADS - 1)
    valid = lane8 == row8
    ux = ux_ref[...]

    def online(s, m_ref, l_ref, a_ref, v_page):
        m_prev = m_ref[...]
        m_new = jnp.maximum(m_prev, jnp.max(s, axis=1, keepdims=True))
        p = jnp.exp(s - _rep(m_new, n_rep))
        alpha = jnp.exp(m_prev - m_new)
        l_ref[...] = alpha * l_ref[...] + jnp.sum(p, axis=1, keepdims=True)
        a_ref[...] = alpha * a_ref[...] + jnp.dot(p.astype(BF16), v_page.astype(BF16),
                                                 preferred_element_type=F32)
        m_ref[...] = m_new

    for r in range(P):
        s_all = lax.dot_general(qf16[...], kc[r][...].astype(BF16), NT_DIMS, preferred_element_type=F32)
        lt = lc[r][0]
        pieces = jnp.concatenate([p.astype(F32) for p in _split3(lt)], axis=0).astype(BF16)
        b3 = jnp.dot(pieces, ux, preferred_element_type=F32)
        bias = dup(b3[0:N_HEADS] + b3[N_HEADS:2 * N_HEADS] + b3[2 * N_HEADS:3 * N_HEADS]) + _rep(cf[...], n_rep)
        online(jnp.where(valid, s_all + bias, NEG_INF), mf, lf, af, vc[r][...])
        cf[...] = cf[...] + dup(jnp.sum(lt, axis=1, keepdims=True))
        sd_all = lax.dot_general(qd16[...], dkc[r][...].astype(BF16), NT_DIMS, preferred_element_type=F32)
        online(jnp.where(valid, sd_all, NEG_INF), md, ld, ad, dvc[r][...])

    @pl.when(j == pl.num_programs(1) - 1)
    def _():
        of_ref[0] = (af[...] / lf[...])[0:N_HEADS]
        o = ad[...] / ld[...]
        od_ref[0] = _diff_finish(o[0:N_HEADS], o[N_HEADS:], _diff_lambda(lamp_ref), gn_ref[...])


def _decode_attention(page_table, small, ux, lamp, gn, caches, lf_cache_t, pages_per_step):
    NS, n_pages = page_table.shape
    P = pages_per_step
    page_rows = caches[0].shape[0] // lf_cache_t.shape[0]
    per_seq = pl.BlockSpec((1, N_HEADS, D_HEAD), lambda b, j, pt: (b, 0, 0))

    def page_spec(r):
        return pl.BlockSpec((page_rows, D_HEAD), lambda b, j, pt: (pt[b, n_pages - 1 - (j * P + r)], 0))

    def lf_spec(r):
        return pl.BlockSpec((1, N_HEADS, LANE), lambda b, j, pt: (pt[b, n_pages - 1 - (j * P + r)], 0, 0))

    kc, vc, dkc, dvc = caches
    in_specs = ([per_seq] * 7
                + [pl.BlockSpec(ux.shape, lambda b, j, pt: (0, 0)),
                   pl.BlockSpec((8, LANE), lambda b, j, pt: (0, 0)),
                   pl.BlockSpec((1, LANE), lambda b, j, pt: (0, 0))]
                + [page_spec(r) for r in range(P)] * 2
                + [lf_spec(r) for r in range(P)]
                + [page_spec(r) for r in range(P)] * 2)
    args = (list(small) + [ux, lamp, gn] + [kc] * P + [vc] * P + [lf_cache_t] * P + [dkc] * P + [dvc] * P)
    grid_spec = pltpu.PrefetchScalarGridSpec(
        num_scalar_prefetch=1,
        grid=(NS, n_pages // P),
        in_specs=in_specs,
        out_specs=[per_seq, per_seq],
        scratch_shapes=(
            [pltpu.VMEM((2 * N_HEADS, LANE), F32)] * 4 + [pltpu.VMEM((2 * N_HEADS, D_HEAD), BF16)]
            + [pltpu.VMEM((2 * N_HEADS, LANE), F32)] * 3 + [pltpu.VMEM((2 * N_HEADS, D_HEAD), BF16)]),
    )
    return pl.pallas_call(
        functools.partial(_decode_kernel, pages_per_step=P),
        grid_spec=grid_spec,
        out_shape=[jax.ShapeDtypeStruct((NS, N_HEADS, D_HEAD), F32)] * 2,
        compiler_params=pltpu.CompilerParams(
            dimension_semantics=("arbitrary", "arbitrary"), vmem_limit_bytes=VMEM_LIMIT),
        name="paged_decode",
    )(page_table, *args)


def _mix_kernel(h_ref, of_ref, od_ref, wt_ref, wb_ref, g_ref, b_ref, o_ref):
    y = (ALPHA * h_ref[...]
         + jnp.dot(of_ref[...], wt_ref[...], preferred_element_type=F32)
         + jnp.dot(od_ref[...], wb_ref[...], preferred_element_type=F32))
    o_ref[...] = _layer_norm(y, g_ref[...], b_ref[...])


def _mix_out(h, o_f, o_d, wo_top, wo_bot, g, b, tm):
    R, D = h.shape
    row = lambda i: (i, 0)
    const = lambda i: (0, 0)
    return pl.pallas_call(
        _mix_kernel,
        grid=(R // tm,),
        in_specs=[pl.BlockSpec((tm, D), row), pl.BlockSpec((tm, GROUP), row), pl.BlockSpec((tm, GROUP), row),
                  pl.BlockSpec((GROUP, D), const), pl.BlockSpec((GROUP, D), const),
                  pl.BlockSpec((1, D), const), pl.BlockSpec((1, D), const)],
        out_specs=pl.BlockSpec((tm, D), row),
        out_shape=jax.ShapeDtypeStruct((R, D), F32),
        compiler_params=pltpu.CompilerParams(dimension_semantics=("arbitrary",), vmem_limit_bytes=VMEM_LIMIT),
        name="mix_out_ln1",
    )(h, o_f, o_d, wo_top, wo_bot, g, b)


def _ffn_kernel(h_ref, wu_ref, wd_ref, g_ref, b_ref, o_ref, hb_scr, acc_scr):
    j = pl.program_id(1)

    @pl.when(j == 0)
    def _():
        hb_scr[...] = h_ref[...].astype(BF16)
        acc_scr[...] = jnp.zeros_like(acc_scr)

    u = jnp.maximum(jnp.dot(hb_scr[...], wu_ref[...], preferred_element_type=F32), 0.0)
    acc_scr[...] += jnp.dot((u * u).astype(BF16), wd_ref[...], preferred_element_type=F32)

    @pl.when(j == pl.num_programs(1) - 1)
    def _():
        o_ref[...] = _layer_norm(ALPHA * h_ref[...] + acc_scr[...], g_ref[...], b_ref[...])


def _ffn(h1, w_up, w_down, g, b, tm, tf):
    R, D = h1.shape
    d_ff = w_up.shape[1]
    return pl.pallas_call(
        _ffn_kernel,
        grid=(R // tm, d_ff // tf),
        in_specs=[pl.BlockSpec((tm, D), lambda i, j: (i, 0)),
                  pl.BlockSpec((D, tf), lambda i, j: (0, j)),
                  pl.BlockSpec((tf, D), lambda i, j: (j, 0)),
                  pl.BlockSpec((1, D), lambda i, j: (0, 0)),
                  pl.BlockSpec((1, D), lambda i, j: (0, 0))],
        out_specs=pl.BlockSpec((tm, D), lambda i, j: (i, 0)),
        out_shape=jax.ShapeDtypeStruct((R, D), F32),
        scratch_shapes=[pltpu.VMEM((tm, D), BF16), pltpu.VMEM((tm, D), F32)],
        compiler_params=pltpu.CompilerParams(
            dimension_semantics=("arbitrary", "arbitrary"), vmem_limit_bytes=VMEM_LIMIT),
        name="ffn_ln2",
    )(h1, w_up, w_down, g, b)


def _rope_tables(pos):
    half = ROT_DIM // 2
    inv_freq = jnp.power(jnp.float32(ROPE_THETA), -jnp.arange(0, ROT_DIM, 2, dtype=F32) / ROT_DIM)
    ang = pos.astype(F32)[:, None] * inv_freq[None, :]
    cos, sin = jnp.cos(ang), jnp.sin(ang)
    lane = jnp.arange(D_HEAD) % D_MAP
    f = lane % half
    first, second = lane < half, (lane >= half) & (lane < ROT_DIM)
    cos_t = jnp.where((first | second)[None, :], cos[:, f], 1.0)
    sa_t = jnp.where(first[None, :], -sin[:, f], 0.0)
    sb_t = jnp.where(second[None, :], sin[:, f], 0.0)
    return cos_t, sa_t, sb_t


def kernel(x_prompt, x_sample, cache_fox_k, cache_fox_v, cache_fox_logf, cache_diff_k, cache_diff_v, page_table, meta_tokens, ln_in_g, ln_in_b, w_in, b_forget, lambda_q1, lambda_k1, lambda_q2, lambda_k2, diff_norm_g, w_o, ln1_g, ln1_b, w_up, w_down, ln2_g, ln2_b):
    batch, seq, D = x_prompt.shape
    NS, dec_seq, _ = x_sample.shape
    depth, n_pool, page_size = cache_fox_k.shape[:3]
    n_pages = page_table.shape[1]
    assert batch == 1 and dec_seq == 1 and depth == DEPTH
    assert cache_fox_k.shape[3:] == (N_HEADS, D_HEAD) and meta_tokens.shape[0] == N_META
    past_len = n_pages * page_size

    Lr = N_META + seq
    S0 = _round_up(Lr, 32)
    R = _round_up(S0 + NS, LANE)
    tm = _pick_tile(R, (640, 512, 384, 256, 128))
    tm_mix = _pick_tile(R, (320, 256, 128))
    x_all = jnp.concatenate([meta_tokens.astype(F32), x_prompt[0], jnp.zeros((S0 - Lr, D), F32),
                             x_sample[:, 0], jnp.zeros((R - S0 - NS, D), F32)], axis=0)
    pos = jnp.concatenate([jnp.arange(Lr), jnp.zeros((S0 - Lr,), jnp.int32),
                           jnp.full((NS,), past_len), jnp.zeros((R - S0 - NS,), jnp.int32)])
    cos_t, sa_t, sb_t = _rope_tables(pos)

    fq, fk, fv, wf, dq, dk, dv = jnp.split(w_in[0], [GROUP, 2 * GROUP, 3 * GROUP, 3 * GROUP + N_HEADS,
                                                     4 * GROUP + N_HEADS, 5 * GROUP + N_HEADS], axis=1)
    w_main = jnp.concatenate([fq, fk, fv, dq, dk, dv], axis=1).astype(BF16)
    w_f = jnp.repeat(wf, BIAS_LANES, axis=1).astype(BF16)
    b_f = jnp.repeat(b_forget[0].astype(F32), BIAS_LANES)[None, :]
    row2 = lambda v: v.astype(F32).reshape(1, -1)

    h, zf, zb, qb, kb, lf = _project(x_all, row2(ln_in_g), row2(ln_in_b), w_main, w_f, b_f,
                                     cos_t, sa_t, sb_t, tm)

    lamp = jnp.zeros((8, LANE), F32).at[0:4, 0:D_MAP].set(
        jnp.stack([lambda_q1[0], lambda_k1[0], lambda_q2[0], lambda_k2[0]]).astype(F32))
    gn = row2(diff_norm_g[0])
    o_f = _fox_attention(zb, qb, kb, tm)
    o_d = _diff_attention(zb, lamp, gn, tm)

    grp = lambda a, g: a[S0:S0 + NS, g * GROUP:(g + 1) * GROUP].reshape(NS, N_HEADS, D_HEAD).astype(F32)
    lf_rows = lf[:, ::BIAS_LANES]
    lf_own = jnp.broadcast_to(lf_rows[S0:S0 + NS, :, None], (NS, N_HEADS, LANE))
    small = (grp(zb, 0), grp(zb, 1), grp(zb, 2), lf_own, grp(zb, 3), grp(zb, 4), grp(zb, 5))
    page_rows = page_size * N_HEADS
    flat = lambda c: c[0].reshape(n_pool * page_rows, D_HEAD)
    lf_cache_t = jnp.swapaxes(cache_fox_logf[0], 1, 2)
    s_idx = jnp.arange(page_size)
    ux = (s_idx[:, None] > jnp.repeat(s_idx, N_HEADS)[None, :]).astype(BF16)
    P = 4 if n_pages % 4 == 0 else 1
    os_f, os_d = _decode_attention(page_table, small, ux, lamp, gn,
                                   (flat(cache_fox_k), flat(cache_fox_v), flat(cache_diff_k), flat(cache_diff_v)),
                                   lf_cache_t, P)
    o_f = o_f.at[S0:S0 + NS].set(os_f.reshape(NS, GROUP).astype(BF16))
    o_d = o_d.at[S0:S0 + NS].set(os_d.reshape(NS, GROUP).astype(BF16))

    wo = w_o[0].astype(BF16)
    h1 = _mix_out(h, o_f, o_d, wo[:GROUP], wo[GROUP:], row2(ln1_g[0]), row2(ln1_b[0]), tm_mix)
    tf = _pick_tile(w_up.shape[2], (512, 256, 128))
    y = _ffn(h1, w_up[0].astype(BF16), w_down[0].astype(BF16), row2(ln2_g[0]), row2(ln2_b[0]), tm, tf)

    heads = lambda a, g, lo, n: a[lo:lo + n, g * GROUP:(g + 1) * GROUP].reshape(n, N_HEADS, D_HEAD)
    y_prompt = y[N_META:Lr][None]
    y_sample = y[S0:S0 + NS][:, None]
    p_out = (heads(zf, 1, 0, Lr)[None, None], heads(zf, 2, 0, Lr)[None, None], lf_rows[:Lr][None, None],
             heads(zf, 4, 0, Lr)[None, None], heads(zf, 5, 0, Lr)[None, None])
    s_out = (heads(zf, 1, S0, NS)[None, :, None], heads(zf, 2, S0, NS)[None, :, None],
             lf_rows[S0:S0 + NS][None, :, None],
             heads(zf, 4, S0, NS)[None, :, None], heads(zf, 5, S0, NS)[None, :, None])
    return (y_prompt, y_sample) + p_out + s_out
```

```python
import functools
import math

import jax
import jax.numpy as jnp
from jax import lax
from jax.experimental import pallas as pl
from jax.experimental.pallas import tpu as pltpu

F32 = jnp.float32
BF16 = jnp.bfloat16

N_META = 16
N_HEADS = 8
D_HEAD = 128
D_MAP = 64
ROT_DIM = D_MAP // 4
ROPE_THETA = 500000.0
DEPTH = 1
ALPHA = (2 * DEPTH) ** 0.25
LN_EPS = 1e-5
NEG_INF = -1e30
LAM_INIT = 0.8 - 0.6 * math.exp(-0.3 * 0)
FOX_SCALE = D_HEAD ** -0.5
DIFF_SCALE = D_MAP ** -0.5
GROUP = N_HEADS * D_HEAD
N_GROUPS = 6
BIAS_LANES = D_HEAD // N_HEADS

LANE = 128
SUBLANE = 8
MXU_DEPTH = 256
VMEM_LIMIT = 56 * 1024 * 1024
DIAG_BLOCKS = 3
ROW_TILE = DIAG_BLOCKS * MXU_DEPTH

NT_DIMS = (((1,), (1,)), ((), ()))


def _round_up(x, m):
    return (x + m - 1) // m * m


def _layer_norm(x, g, b):
    mu = jnp.mean(x, -1, keepdims=True)
    xc = x - mu
    var = jnp.mean(xc * xc, -1, keepdims=True)
    return xc * lax.rsqrt(var + LN_EPS) * g + b


def _split3(x):
    hi = x.astype(BF16).astype(F32)
    r1 = x - hi
    mid = r1.astype(BF16).astype(F32)
    lo = (r1 - mid).astype(BF16).astype(F32)
    return hi, mid, lo


def _rep(x, n):
    return jnp.concatenate([x] * n, axis=1)


def _gate_kernel(x_ref, g_ref, b_ref, wf_ref, bf_ref, hb_ref, qb_ref, kb_ref, lf_ref, carry_scr, *, tm):
    i = pl.program_id(0)
    hb = _layer_norm(x_ref[...], g_ref[...], b_ref[...]).astype(BF16)
    hb_ref[...] = hb
    ff = jnp.dot(hb, wf_ref[...], preferred_element_type=F32) + bf_ref[...]
    lf = jnp.minimum(ff, 0.0) - jnp.log1p(jnp.exp(-jnp.abs(ff)))
    lf_ref[...] = lf

    @pl.when(i == 0)
    def _():
        carry_scr[...] = jnp.zeros_like(carry_scr)

    row = lax.broadcasted_iota(jnp.int32, (tm, tm), 0)
    col = lax.broadcasted_iota(jnp.int32, (tm, tm), 1)
    tri = (col <= row).astype(F32)
    cs = jnp.dot(tri, lf, precision=lax.Precision.HIGHEST, preferred_element_type=F32) + carry_scr[0:1, :]
    carry_scr[...] = jnp.broadcast_to(cs[tm - 1:tm, :], carry_scr.shape)
    hi, mid, lo = _split3(cs)
    lane = lax.broadcasted_iota(jnp.int32, (tm, LANE), 1) & (BIAS_LANES - 1)
    one = jnp.ones((tm, LANE), F32)
    zero = jnp.zeros((tm, LANE), F32)
    qb_ref[...] = jnp.where(lane == 0, hi, jnp.where(lane == 1, mid, jnp.where(
        lane == 2, lo, jnp.where(lane < 6, one, zero)))).astype(BF16)
    kb_ref[...] = jnp.where(lane < 3, one, jnp.where(lane == 3, -hi, jnp.where(
        lane == 4, -mid, jnp.where(lane == 5, -lo, zero)))).astype(BF16)


def _ln_gate(x_all, ln_g, ln_b, w_f, b_f, tm):
    R, D = x_all.shape
    row = lambda i: (i, 0)
    const = lambda i: (0, 0)
    return pl.pallas_call(
        functools.partial(_gate_kernel, tm=tm),
        grid=(R // tm,),
        in_specs=[pl.BlockSpec((tm, D), row), pl.BlockSpec((1, D), const), pl.BlockSpec((1, D), const),
                  pl.BlockSpec((D, LANE), const), pl.BlockSpec((1, LANE), const)],
        out_specs=[pl.BlockSpec((tm, D), row), pl.BlockSpec((tm, LANE), row),
                   pl.BlockSpec((tm, LANE), row), pl.BlockSpec((tm, LANE), row)],
        out_shape=[jax.ShapeDtypeStruct((R, D), BF16),
                   jax.ShapeDtypeStruct((R, LANE), BF16),
                   jax.ShapeDtypeStruct((R, LANE), BF16),
                   jax.ShapeDtypeStruct((R, LANE), F32)],
        scratch_shapes=[pltpu.VMEM((SUBLANE, LANE), F32)],
        compiler_params=pltpu.CompilerParams(dimension_semantics=("arbitrary",), vmem_limit_bytes=VMEM_LIMIT),
        name="ln_gate",
    )(x_all, ln_g, ln_b, w_f, b_f)


def _rope(zh, cos, sa, sb):
    return zh * cos + pltpu.roll(zh, D_HEAD - ROT_DIM // 2, 1) * sa + pltpu.roll(zh, ROT_DIM // 2, 1) * sb


def _proj_kernel(hb_ref, wm_ref, cos_ref, sa_ref, sb_ref, zb_ref, pk_ref, pv_ref, pdk_ref, pdv_ref,
                 sk_ref, sv_ref, sdk_ref, sdv_ref, *, tm, s_tile, s_off, ns):
    j = pl.program_id(0)
    i = pl.program_id(1)
    heads = [slice(h * D_HEAD, (h + 1) * D_HEAD) for h in range(N_HEADS)]

    def matmul():
        return jnp.dot(hb_ref[...], wm_ref[...], preferred_element_type=F32)

    def cache_store(per_head, p_ref, s_ref):
        for h in range(N_HEADS):
            p_ref[pl.ds(h, tm, stride=N_HEADS), :] = per_head[h]

        @pl.when(i == s_tile)
        def _():
            for h in range(N_HEADS):
                s_ref[pl.ds(h, ns, stride=N_HEADS), :] = per_head[h][s_off:s_off + ns]

    @pl.when(j == 0)
    def _():
        zb_ref[...] = (matmul() * FOX_SCALE).astype(BF16)

    def plain(p_ref, s_ref):
        z = matmul()
        zb_ref[...] = z.astype(BF16)
        cache_store([z[:, hs] for hs in heads], p_ref, s_ref)

    pl.when(j == 1)(functools.partial(plain, pk_ref, sk_ref))
    pl.when(j == 2)(functools.partial(plain, pv_ref, sv_ref))
    pl.when(j == 5)(functools.partial(plain, pdv_ref, sdv_ref))

    @pl.when(j == 3)
    def _():
        z = matmul()
        cos, sa, sb = cos_ref[...], sa_ref[...], sb_ref[...]
        for hs in heads:
            zb_ref[:, hs] = (_rope(z[:, hs], cos, sa, sb) * DIFF_SCALE).astype(BF16)

    @pl.when(j == 4)
    def _():
        z = matmul()
        cos, sa, sb = cos_ref[...], sa_ref[...], sb_ref[...]
        ys = [_rope(z[:, hs], cos, sa, sb) for hs in heads]
        for hs, y in zip(heads, ys):
            zb_ref[:, hs] = y.astype(BF16)
        cache_store(ys, pdk_ref, sdk_ref)


def _project(hb, w_main, cos_t, sa_t, sb_t, tm, n_prompt_rows, s0, ns):
    R, D = hb.shape
    n = R // tm
    row = lambda j, i: (i, 0)

    def cache_spec(group):
        return pl.BlockSpec((tm * N_HEADS, D_HEAD),
                            lambda j, i: (jnp.where(j < group, 0, jnp.where(j == group, i, n - 1)), 0))

    sample_spec = pl.BlockSpec((ns * N_HEADS, D_HEAD), lambda j, i: (0, 0))
    cache_shape = jax.ShapeDtypeStruct((n_prompt_rows * N_HEADS, D_HEAD), F32)
    sample_shape = jax.ShapeDtypeStruct((ns * N_HEADS, D_HEAD), F32)
    return pl.pallas_call(
        functools.partial(_proj_kernel, tm=tm, s_tile=s0 // tm, s_off=s0 % tm, ns=ns),
        grid=(N_GROUPS, n),
        in_specs=[pl.BlockSpec((tm, D), row),
                  pl.BlockSpec((D, GROUP), lambda j, i: (0, j)),
                  pl.BlockSpec((tm, LANE), row), pl.BlockSpec((tm, LANE), row), pl.BlockSpec((tm, LANE), row)],
        out_specs=[pl.BlockSpec((tm, GROUP), lambda j, i: (i, j)),
                   cache_spec(1), cache_spec(2), cache_spec(4), cache_spec(5),
                   sample_spec, sample_spec, sample_spec, sample_spec],
        out_shape=[jax.ShapeDtypeStruct((R, N_GROUPS * GROUP), BF16),
                   cache_shape, cache_shape, cache_shape, cache_shape,
                   sample_shape, sample_shape, sample_shape, sample_shape],
        compiler_params=pltpu.CompilerParams(
            dimension_semantics=("arbitrary", "arbitrary"), vmem_limit_bytes=VMEM_LIMIT),
        name="in_proj",
    )(hb, w_main, cos_t, sa_t, sb_t)


def _flash_block(qs_scr, k_ref, kb_ref, v_ref, m_scr, acc_scr, h, r0, nr, nc, keep):
    hs = slice(h * D_HEAD, (h + 1) * D_HEAD)
    kh = k_ref[0:nc, hs]
    if kb_ref is not None:
        kh = jnp.concatenate([kh, kb_ref[0:nc, :]], axis=1)
    s = lax.dot_general(qs_scr[h, r0:r0 + nr, :], kh, NT_DIMS, preferred_element_type=F32)
    if keep is not None:
        s = jnp.where(keep, s, NEG_INF)
    m_prev = m_scr[h, r0:r0 + nr, :]
    m_new = jnp.maximum(m_prev, jnp.max(s, axis=1, keepdims=True))
    p = jnp.exp(s - _rep(m_new, nc // LANE))
    alpha = jnp.exp(m_prev - m_new)
    vh = jnp.concatenate([v_ref[0:nc, hs], jnp.ones((nc, D_HEAD), BF16)], axis=1)
    pv = jnp.dot(p.astype(BF16), vh, preferred_element_type=F32)
    acc_scr[h, r0:r0 + nr, :] = acc_scr[h, r0:r0 + nr, :] * _rep(alpha, 2) + pv
    m_scr[h, r0:r0 + nr, :] = m_new


def _flash_full(qs_scr, k_ref, kb_ref, v_ref, m_scr, acc_scr, *, tq, nmap):
    for h in range(N_HEADS):
        _flash_block(qs_scr, k_ref, kb_ref, v_ref, m_scr, acc_scr, h, 0, nmap * tq, tq, None)


def _flash_diag(qs_scr, k_ref, kb_ref, v_ref, m_scr, acc_scr, *, tq, nmap):
    sub = tq // DIAG_BLOCKS
    for h in range(N_HEADS):
        for rb in range(DIAG_BLOCKS):
            nc = (rb + 1) * sub
            row = lax.broadcasted_iota(jnp.int32, (sub, nc), 0) + rb * sub
            keep = lax.broadcasted_iota(jnp.int32, (sub, nc), 1) <= row
            for mp in range(nmap):
                _flash_block(qs_scr, k_ref, kb_ref, v_ref, m_scr, acc_scr, h, mp * tq + rb * sub, sub, nc, keep)


def _flash_init(m_scr, acc_scr):
    m_scr[...] = jnp.full(m_scr.shape, NEG_INF, F32)
    acc_scr[...] = jnp.zeros(acc_scr.shape, F32)


def _fox_kernel(qi_tab, ki_tab, q_ref, k_ref, v_ref, qb_ref, kb_ref, o_ref, qs_scr, m_scr, acc_scr, *, tq):
    s = pl.program_id(0)
    qi = qi_tab[s]
    ki = ki_tab[s]

    @pl.when(ki == 0)
    def _():
        _flash_init(m_scr, acc_scr)
        qb = qb_ref[...].astype(F32)
        head_of_lane = lax.broadcasted_iota(jnp.int32, qb.shape, 1) // BIAS_LANES
        for h in range(N_HEADS):
            qbh = jnp.where(head_of_lane == h, qb, 0.0).astype(BF16)
            qs_scr[h] = jnp.concatenate([q_ref[:, h * D_HEAD:(h + 1) * D_HEAD], qbh], axis=1)

    args = (qs_scr, k_ref, kb_ref, v_ref, m_scr, acc_scr)

    @pl.when(ki != qi)
    def _():
        _flash_full(*args, tq=tq, nmap=1)

    @pl.when(ki == qi)
    def _():
        _flash_diag(*args, tq=tq, nmap=1)
        for h in range(N_HEADS):
            a = acc_scr[h]
            o_ref[:, h * D_HEAD:(h + 1) * D_HEAD] = (a[:, :D_HEAD] / a[:, D_HEAD:]).astype(o_ref.dtype)


def _diff_lambda(lamp_ref):
    lp = lamp_ref[...]
    return (jnp.exp(jnp.sum(lp[0:1] * lp[1:2], axis=1, keepdims=True))
            - jnp.exp(jnp.sum(lp[2:3] * lp[3:4], axis=1, keepdims=True)) + LAM_INIT)


def _diff_finish(o1, o2, lam, gn):
    od = o1 - lam * o2
    od = od * lax.rsqrt(jnp.mean(od * od, axis=-1, keepdims=True) + LN_EPS)
    return od * gn * (1.0 - LAM_INIT)


def _diff_kernel(qi_tab, ki_tab, q_ref, k_ref, v_ref, lamp_ref, gn_ref, o_ref, qs_scr, m_scr, acc_scr, *, tq):
    s = pl.program_id(0)
    qi = qi_tab[s]
    ki = ki_tab[s]

    @pl.when(ki == 0)
    def _():
        _flash_init(m_scr, acc_scr)
        lane = lax.broadcasted_iota(jnp.int32, (tq, D_HEAD), 1)
        for h in range(N_HEADS):
            qh = q_ref[:, h * D_HEAD:(h + 1) * D_HEAD].astype(F32)
            qs_scr[h, 0:tq, :] = jnp.where(lane < D_MAP, qh, 0.0).astype(BF16)
            qs_scr[h, tq:2 * tq, :] = jnp.where(lane >= D_MAP, qh, 0.0).astype(BF16)

    args = (qs_scr, k_ref, None, v_ref, m_scr, acc_scr)

    @pl.when(ki != qi)
    def _():
        _flash_full(*args, tq=tq, nmap=2)

    @pl.when(ki == qi)
    def _():
        _flash_diag(*args, tq=tq, nmap=2)
        lam = _diff_lambda(lamp_ref)
        gn = gn_ref[...]
        for h in range(N_HEADS):
            a = acc_scr[h]
            o1 = a[0:tq, :D_HEAD] / a[0:tq, D_HEAD:]
            o2 = a[tq:2 * tq, :D_HEAD] / a[tq:2 * tq, D_HEAD:]
            o_ref[:, h * D_HEAD:(h + 1) * D_HEAD] = _diff_finish(o1, o2, lam, gn).astype(o_ref.dtype)


def _causal_steps(n_tiles):
    qi = [q for q in range(n_tiles) for _ in range(q + 1)]
    ki = [k for q in range(n_tiles) for k in range(q + 1)]
    return jnp.asarray(qi, jnp.int32), jnp.asarray(ki, jnp.int32)


def _fox_attention(zb, qb, kb, tq):
    R = zb.shape[0]
    qi_tab, ki_tab = _causal_steps(R // tq)
    grid_spec = pltpu.PrefetchScalarGridSpec(
        num_scalar_prefetch=2,
        grid=(qi_tab.shape[0],),
        in_specs=[
            pl.BlockSpec((tq, GROUP), lambda s, qt, kt: (qt[s], 0)),
            pl.BlockSpec((tq, GROUP), lambda s, qt, kt: (kt[s], 1)),
            pl.BlockSpec((tq, GROUP), lambda s, qt, kt: (kt[s], 2)),
            pl.BlockSpec((tq, LANE), lambda s, qt, kt: (qt[s], 0)),
            pl.BlockSpec((tq, LANE), lambda s, qt, kt: (kt[s], 0)),
        ],
        out_specs=pl.BlockSpec((tq, GROUP), lambda s, qt, kt: (qt[s], 0)),
        scratch_shapes=[
            pltpu.VMEM((N_HEADS, tq, 2 * D_HEAD), BF16),
            pltpu.VMEM((N_HEADS, tq, LANE), F32),
            pltpu.VMEM((N_HEADS, tq, 2 * D_HEAD), F32),
        ],
    )
    return pl.pallas_call(
        functools.partial(_fox_kernel, tq=tq),
        grid_spec=grid_spec,
        out_shape=jax.ShapeDtypeStruct((R, GROUP), BF16),
        compiler_params=pltpu.CompilerParams(dimension_semantics=("arbitrary",), vmem_limit_bytes=VMEM_LIMIT),
        name="fox_flash",
    )(qi_tab, ki_tab, zb, zb, zb, qb, kb)


def _diff_attention(zb, lamp, gn, tq):
    R = zb.shape[0]
    qi_tab, ki_tab = _causal_steps(R // tq)
    grid_spec = pltpu.PrefetchScalarGridSpec(
        num_scalar_prefetch=2,
        grid=(qi_tab.shape[0],),
        in_specs=[
            pl.BlockSpec((tq, GROUP), lambda s, qt, kt: (qt[s], 3)),
            pl.BlockSpec((tq, GROUP), lambda s, qt, kt: (kt[s], 4)),
            pl.BlockSpec((tq, GROUP), lambda s, qt, kt: (kt[s], 5)),
            pl.BlockSpec((SUBLANE, LANE), lambda s, qt, kt: (0, 0)),
            pl.BlockSpec((1, LANE), lambda s, qt, kt: (0, 0)),
        ],
        out_specs=pl.BlockSpec((tq, GROUP), lambda s, qt, kt: (qt[s], 0)),
        scratch_shapes=[
            pltpu.VMEM((N_HEADS, 2 * tq, D_HEAD), BF16),
            pltpu.VMEM((N_HEADS, 2 * tq, LANE), F32),
            pltpu.VMEM((N_HEADS, 2 * tq, 2 * D_HEAD), F32),
        ],
    )
    return pl.pallas_call(
        functools.partial(_diff_kernel, tq=tq),
        grid_spec=grid_spec,
        out_shape=jax.ShapeDtypeStruct((R, GROUP), BF16),
        compiler_params=pltpu.CompilerParams(dimension_semantics=("arbitrary",), vmem_limit_bytes=VMEM_LIMIT),
        name="diff_flash",
    )(qi_tab, ki_tab, zb, zb, zb, lamp, gn)


def _decode_kernel(pt_ref, *refs, pages_per_step):
    P = pages_per_step
    (qf_ref, kf_ref, vf_ref, lfo_ref, qd_ref, kd_ref, vd_ref, ux_ref, lamp_ref, gn_ref) = refs[:10]
    kc = refs[10:10 + P]
    vc = refs[10 + P:10 + 2 * P]
    lc = refs[10 + 2 * P:10 + 3 * P]
    dkc = refs[10 + 3 * P:10 + 4 * P]
    dvc = refs[10 + 4 * P:10 + 5 * P]
    of_ref, od_ref = refs[10 + 5 * P:12 + 5 * P]
    mf, lf, af, cf, qf16, md, ld, ad, qd16 = refs[12 + 5 * P:]
    j = pl.program_id(1)
    page_rows = kc[0].shape[0]
    n_rep = page_rows // LANE
    dup = lambda x: jnp.concatenate([x, x], axis=0)

    @pl.when(j == 0)
    def _():
        q2 = dup(qf_ref[0])
        qf16[...] = q2.astype(BF16)
        s_own = jnp.sum(q2 * dup(kf_ref[0]), axis=1, keepdims=True)
        mf[...] = jnp.broadcast_to(s_own, mf.shape)
        lf[...] = jnp.ones(lf.shape, F32)
        af[...] = dup(vf_ref[0])
        cf[...] = dup(lfo_ref[0])
        lane = lax.broadcasted_iota(jnp.int32, (N_HEADS, D_HEAD), 1)
        qd = qd_ref[0]
        q16 = jnp.concatenate([jnp.where(lane < D_MAP, qd, 0.0), jnp.where(lane >= D_MAP, qd, 0.0)], axis=0)
        qd16[...] = q16.astype(BF16)
        sd_own = jnp.sum(q16 * dup(kd_ref[0]), axis=1, keepdims=True)
        md[...] = jnp.broadcast_to(sd_own, md.shape)
        ld[...] = jnp.ones(ld.shape, F32)
        ad[...] = dup(vd_ref[0])

    lane8 = lax.broadcasted_iota(jnp.int32, (2 * N_HEADS, page_rows), 1) & (N_HEADS - 1)
    row8 = lax.broadcasted_iota(jnp.int32, (2 * N_HEADS, page_rows), 0) & (N_HEADS - 1)
    valid = lane8 == row8
    ux = ux_ref[...]

    def online(scores, m_ref, l_ref, a_ref, v_pages):
        m_prev = m_ref[...]
        m_new = m_prev
        for s in scores:
            m_new = jnp.maximum(m_new, jnp.max(s, axis=1, keepdims=True))
        alpha = jnp.exp(m_prev - m_new)
        m_rep = _rep(m_new, n_rep)
        l_new = alpha * l_ref[...]
        a_new = alpha * a_ref[...]
        for s, v_page in zip(scores, v_pages):
            p = jnp.exp(s - m_rep)
            l_new = l_new + jnp.sum(p, axis=1, keepdims=True)
            a_new = a_new + jnp.dot(p.astype(BF16), v_page[...].astype(BF16), preferred_element_type=F32)
        l_ref[...] = l_new
        a_ref[...] = a_new
        m_ref[...] = m_new

    carry = cf[...]
    fox_scores = []
    for r in range(P):
        s_all = lax.dot_general(qf16[...], kc[r][...].astype(BF16), NT_DIMS, preferred_element_type=F32)
        lt = lc[r][0]
        pieces = jnp.concatenate(_split3(lt), axis=0).astype(BF16)
        b3 = jnp.dot(pieces, ux, preferred_element_type=F32)
        bias = dup(b3[0:N_HEADS] + b3[N_HEADS:2 * N_HEADS] + b3[2 * N_HEADS:3 * N_HEADS]) + _rep(carry, n_rep)
        fox_scores.append(jnp.where(valid, s_all + bias, NEG_INF))
        carry = carry + dup(jnp.sum(lt, axis=1, keepdims=True))
    cf[...] = carry
    online(fox_scores, mf, lf, af, vc)

    diff_scores = [jnp.where(valid, lax.dot_general(qd16[...], dkc[r][...].astype(BF16), NT_DIMS,
                                                    preferred_element_type=F32), NEG_INF) for r in range(P)]
    online(diff_scores, md, ld, ad, dvc)

    @pl.when(j == pl.num_programs(1) - 1)
    def _():
        of_ref[0] = (af[...] / lf[...])[0:N_HEADS]
        o = ad[...] / ld[...]
        od_ref[0] = _diff_finish(o[0:N_HEADS], o[N_HEADS:], _diff_lambda(lamp_ref), gn_ref[...])


def _decode_attention(page_table, small, ux, lamp, gn, caches, lf_cache_t, pages_per_step):
    NS, n_pages = page_table.shape
    P = pages_per_step
    page_rows = caches[0].shape[0] // lf_cache_t.shape[0]
    per_seq = pl.BlockSpec((1, N_HEADS, D_HEAD), lambda b, j, pt: (b, 0, 0))

    def page_spec(r):
        return pl.BlockSpec((page_rows, D_HEAD), lambda b, j, pt: (pt[b, n_pages - 1 - (j * P + r)], 0))

    def lf_spec(r):
        return pl.BlockSpec((1, N_HEADS, LANE), lambda b, j, pt: (pt[b, n_pages - 1 - (j * P + r)], 0, 0))

    kc, vc, dkc, dvc = caches
    in_specs = ([per_seq] * 7
                + [pl.BlockSpec(ux.shape, lambda b, j, pt: (0, 0)),
                   pl.BlockSpec((SUBLANE, LANE), lambda b, j, pt: (0, 0)),
                   pl.BlockSpec((1, LANE), lambda b, j, pt: (0, 0))]
                + [page_spec(r) for r in range(P)] * 2
                + [lf_spec(r) for r in range(P)]
                + [page_spec(r) for r in range(P)] * 2)
    args = (list(small) + [ux, lamp, gn] + [kc] * P + [vc] * P + [lf_cache_t] * P + [dkc] * P + [dvc] * P)
    grid_spec = pltpu.PrefetchScalarGridSpec(
        num_scalar_prefetch=1,
        grid=(NS, n_pages // P),
        in_specs=in_specs,
        out_specs=[per_seq, per_seq],
        scratch_shapes=(
            [pltpu.VMEM((2 * N_HEADS, LANE), F32)] * 4 + [pltpu.VMEM((2 * N_HEADS, D_HEAD), BF16)]
            + [pltpu.VMEM((2 * N_HEADS, LANE), F32)] * 3 + [pltpu.VMEM((2 * N_HEADS, D_HEAD), BF16)]),
    )
    return pl.pallas_call(
        functools.partial(_decode_kernel, pages_per_step=P),
        grid_spec=grid_spec,
        out_shape=[jax.ShapeDtypeStruct((NS, N_HEADS, D_HEAD), F32)] * 2,
        compiler_params=pltpu.CompilerParams(
            dimension_semantics=("arbitrary", "arbitrary"), vmem_limit_bytes=VMEM_LIMIT),
        name="paged_decode",
    )(page_table, *args)


def _mix_kernel(x_ref, of_ref, od_ref, sf_ref, sd_ref, wt_ref, wb_ref, gi_ref, bi_ref, g_ref, b_ref, o_ref,
                *, s_tile, s_off, ns):
    def mixed(x, o_f, o_d):
        y = (ALPHA * _layer_norm(x, gi_ref[...], bi_ref[...])
             + jnp.dot(o_f, wt_ref[...], preferred_element_type=F32)
             + jnp.dot(o_d, wb_ref[...], preferred_element_type=F32))
        return _layer_norm(y, g_ref[...], b_ref[...])

    o_ref[...] = mixed(x_ref[...], of_ref[...], od_ref[...])

    @pl.when(pl.program_id(0) == s_tile)
    def _():
        o_ref[s_off:s_off + ns, :] = mixed(x_ref[s_off:s_off + ns, :], sf_ref[...], sd_ref[...])


def _mix_out(x_all, o_f, o_d, os_f, os_d, wo_top, wo_bot, gi, bi, g, b, tm, s0):
    R, D = x_all.shape
    ns = os_f.shape[0]
    row = lambda i: (i, 0)
    const = lambda i: (0, 0)
    return pl.pallas_call(
        functools.partial(_mix_kernel, s_tile=s0 // tm, s_off=s0 % tm, ns=ns),
        grid=(R // tm,),
        in_specs=[pl.BlockSpec((tm, D), row), pl.BlockSpec((tm, GROUP), row), pl.BlockSpec((tm, GROUP), row),
                  pl.BlockSpec((ns, GROUP), const), pl.BlockSpec((ns, GROUP), const),
                  pl.BlockSpec((GROUP, D), const), pl.BlockSpec((GROUP, D), const),
                  pl.BlockSpec((1, D), const), pl.BlockSpec((1, D), const),
                  pl.BlockSpec((1, D), const), pl.BlockSpec((1, D), const)],
        out_specs=pl.BlockSpec((tm, D), row),
        out_shape=jax.ShapeDtypeStruct((R, D), F32),
        compiler_params=pltpu.CompilerParams(dimension_semantics=("arbitrary",), vmem_limit_bytes=VMEM_LIMIT),
        name="mix_out_ln1",
    )(x_all, o_f, o_d, os_f, os_d, wo_top, wo_bot, gi, bi, g, b)


def _ffn_kernel(h_ref, wu_ref, wd_ref, g_ref, b_ref, o_ref, hb_scr, acc_scr):
    j = pl.program_id(1)

    @pl.when(j == 0)
    def _():
        hb_scr[...] = h_ref[...].astype(BF16)
        acc_scr[...] = jnp.zeros_like(acc_scr)

    u = jnp.maximum(jnp.dot(hb_scr[...], wu_ref[...], preferred_element_type=F32), 0.0)
    acc_scr[...] += jnp.dot((u * u).astype(BF16), wd_ref[...], preferred_element_type=F32)

    @pl.when(j == pl.num_programs(1) - 1)
    def _():
        o_ref[...] = _layer_norm(ALPHA * h_ref[...] + acc_scr[...], g_ref[...], b_ref[...])


def _ffn(h1, w_up, w_down, g, b, tm, tf):
    R, D = h1.shape
    d_ff = w_up.shape[1]
    return pl.pallas_call(
        _ffn_kernel,
        grid=(R // tm, d_ff // tf),
        in_specs=[pl.BlockSpec((tm, D), lambda i, j: (i, 0)),
                  pl.BlockSpec((D, tf), lambda i, j: (0, j)),
                  pl.BlockSpec((tf, D), lambda i, j: (j, 0)),
                  pl.BlockSpec((1, D), lambda i, j: (0, 0)),
                  pl.BlockSpec((1, D), lambda i, j: (0, 0))],
        out_specs=pl.BlockSpec((tm, D), lambda i, j: (i, 0)),
        out_shape=jax.ShapeDtypeStruct((R, D), F32),
        scratch_shapes=[pltpu.VMEM((tm, D), BF16), pltpu.VMEM((tm, D), F32)],
        compiler_params=pltpu.CompilerParams(
            dimension_semantics=("arbitrary", "arbitrary"), vmem_limit_bytes=VMEM_LIMIT),
        name="ffn_ln2",
    )(h1, w_up, w_down, g, b)


def _rope_tables(pos):
    half = ROT_DIM // 2
    inv_freq = jnp.power(jnp.float32(ROPE_THETA), -jnp.arange(0, ROT_DIM, 2, dtype=F32) / ROT_DIM)
    ang = pos.astype(F32)[:, None] * inv_freq[None, :]
    cos, sin = jnp.cos(ang), jnp.sin(ang)
    lane = jnp.arange(D_HEAD) % D_MAP
    f = lane % half
    first, second = lane < half, (lane >= half) & (lane < ROT_DIM)
    cos_t = jnp.where((first | second)[None, :], cos[:, f], 1.0)
    sa_t = jnp.where(first[None, :], -sin[:, f], 0.0)
    sb_t = jnp.where(second[None, :], sin[:, f], 0.0)
    return cos_t, sa_t, sb_t


def kernel(x_prompt, x_sample, cache_fox_k, cache_fox_v, cache_fox_logf, cache_diff_k, cache_diff_v, page_table, meta_tokens, ln_in_g, ln_in_b, w_in, b_forget, lambda_q1, lambda_k1, lambda_q2, lambda_k2, diff_norm_g, w_o, ln1_g, ln1_b, w_up, w_down, ln2_g, ln2_b):
    batch, seq, D = x_prompt.shape
    NS, dec_seq, _ = x_sample.shape
    depth, n_pool, page_size = cache_fox_k.shape[:3]
    n_pages = page_table.shape[1]
    assert batch == 1 and dec_seq == 1 and depth == DEPTH
    assert cache_fox_k.shape[3:] == (N_HEADS, D_HEAD) and meta_tokens.shape[0] == N_META
    past_len = n_pages * page_size

    tm = ROW_TILE
    Lr = N_META + seq
    S0 = _round_up(Lr, 32)
    R = _round_up(S0 + NS, tm)
    assert NS % 16 == 0 or NS < 16
    tm_half = tm // 2
    x_all = jnp.concatenate([meta_tokens.astype(F32), x_prompt[0], jnp.zeros((S0 - Lr, D), F32),
                             x_sample[:, 0], jnp.zeros((R - S0 - NS, D), F32)], axis=0)
    pos = jnp.concatenate([jnp.arange(Lr), jnp.zeros((S0 - Lr,), jnp.int32),
                           jnp.full((NS,), past_len), jnp.zeros((R - S0 - NS,), jnp.int32)])
    cos_t, sa_t, sb_t = _rope_tables(pos)

    fq, fk, fv, wf, dq, dk, dv = jnp.split(w_in[0], [GROUP, 2 * GROUP, 3 * GROUP, 3 * GROUP + N_HEADS,
                                                     4 * GROUP + N_HEADS, 5 * GROUP + N_HEADS], axis=1)
    w_main = jnp.concatenate([fq, fk, fv, dq, dk, dv], axis=1).astype(BF16)
    w_f = jnp.repeat(wf, BIAS_LANES, axis=1).astype(BF16)
    b_f = jnp.repeat(b_forget[0].astype(F32), BIAS_LANES)[None, :]
    row2 = lambda v: v.astype(F32).reshape(1, -1)
    gi, bi = row2(ln_in_g), row2(ln_in_b)

    hb, qb, kb, lf = _ln_gate(x_all, gi, bi, w_f, b_f, tm_half)
    zb, p_fk, p_fv, p_dk, p_dv, s_fk, s_fv, s_dk, s_dv = _project(hb, w_main, cos_t, sa_t, sb_t, tm, Lr, S0, NS)

    lamp = jnp.zeros((SUBLANE, LANE), F32).at[0:4, 0:D_MAP].set(
        jnp.stack([lambda_q1[0], lambda_k1[0], lambda_q2[0], lambda_k2[0]]).astype(F32))
    gn = row2(diff_norm_g[0])
    o_f = _fox_attention(zb, qb, kb, tm)
    o_d = _diff_attention(zb, lamp, gn, tm)

    grp = lambda g: zb[S0:S0 + NS, g * GROUP:(g + 1) * GROUP].reshape(NS, N_HEADS, D_HEAD).astype(F32)
    lf_rows = lf[:, ::BIAS_LANES]
    lf_own = jnp.broadcast_to(lf_rows[S0:S0 + NS, :, None], (NS, N_HEADS, LANE))
    small = (grp(0), grp(1), grp(2), lf_own, grp(3), grp(4), grp(5))
    page_rows = page_size * N_HEADS
    flat = lambda c: c[0].reshape(n_pool * page_rows, D_HEAD)
    lf_cache_t = jnp.swapaxes(cache_fox_logf[0], 1, 2)
    s_idx = jnp.arange(page_size)
    ux = (s_idx[:, None] > jnp.repeat(s_idx, N_HEADS)[None, :]).astype(BF16)
    P = 4 if n_pages % 4 == 0 else 1
    os_f, os_d = _decode_attention(page_table, small, ux, lamp, gn,
                                   (flat(cache_fox_k), flat(cache_fox_v), flat(cache_diff_k), flat(cache_diff_v)),
                                   lf_cache_t, P)

    wo = w_o[0].astype(BF16)
    h1 = _mix_out(x_all, o_f, o_d, os_f.reshape(NS, GROUP).astype(BF16), os_d.reshape(NS, GROUP).astype(BF16),
                  wo[:GROUP], wo[GROUP:], gi, bi, row2(ln1_g[0]), row2(ln1_b[0]), tm_half, S0)
    y = _ffn(h1, w_up[0].astype(BF16), w_down[0].astype(BF16), row2(ln2_g[0]), row2(ln2_b[0]), tm, 512)

    y_prompt = y[N_META:Lr][None]
    y_sample = y[S0:S0 + NS][:, None]
    prompt_cache = lambda a: a.reshape(1, 1, Lr, N_HEADS, D_HEAD)
    sample_cache = lambda a: a.reshape(1, NS, 1, N_HEADS, D_HEAD)
    return (y_prompt, y_sample,
            prompt_cache(p_fk), prompt_cache(p_fv), lf_rows[:Lr][None, None], prompt_cache(p_dk), prompt_cache(p_dv),
            sample_cache(s_fk), sample_cache(s_fv), lf_rows[S0:S0 + NS][None, :, None],
            sample_cache(s_dk), sample_cache(s_dv))
```

```python
import functools
import math
from typing import NamedTuple

import jax
import jax.numpy as jnp
import numpy as np
from jax import lax
from jax.experimental import pallas as pl
from jax.experimental.pallas import tpu as pltpu

F32 = jnp.float32
BF16 = jnp.bfloat16

N_META = 16
N_HEADS = 8
D_HEAD = 128
D_MAP = 64
ROT_DIM = D_MAP // 4
ROPE_THETA = 500000.0
DEPTH = 1
ALPHA = (2 * DEPTH) ** 0.25
LN_EPS = 1e-5
NEG_INF = -1e30
LAM_INIT = 0.8 - 0.6 * math.exp(-0.3 * 0)
FOX_SCALE = D_HEAD ** -0.5
DIFF_SCALE = D_MAP ** -0.5
GROUP = N_HEADS * D_HEAD
N_GROUPS = 6
BIAS_BASE = N_HEADS
BIAS_W = 6

LANE = 128
SUBLANE = 8
MXU_DEPTH = 256
VMEM_LIMIT = 56 * 1024 * 1024
DIAG_BLOCKS = 3
ROW_TILE = DIAG_BLOCKS * MXU_DEPTH

NT_DIMS = (((1,), (1,)), ((), ()))


def _round_up(x, m):
    return (x + m - 1) // m * m


def _layer_norm(x, g, b):
    mu = jnp.mean(x, -1, keepdims=True)
    xc = x - mu
    var = jnp.mean(xc * xc, -1, keepdims=True)
    return xc * lax.rsqrt(var + LN_EPS) * g + b


def _split3(x):
    hi = x.astype(BF16).astype(F32)
    r1 = x - hi
    mid = r1.astype(BF16).astype(F32)
    lo = (r1 - mid).astype(BF16).astype(F32)
    return hi, mid, lo


def _rep(x, n):
    return jnp.concatenate([x] * n, axis=1)


class _Rows(NamedTuple):
    seq: int
    ns: int
    lr: int
    s0: int
    r: int
    tmh: int

    @property
    def n_half(self):
        return self.r // self.tmh


def _x_block_spec(rows, d):
    def index(i, *_):
        return pl.multiple_of(jnp.clip(i * rows.tmh - N_META, 0, rows.seq - rows.tmh), SUBLANE), 0
    return pl.BlockSpec((pl.Element(rows.tmh), pl.Element(d)), index)


def _row_space_x(case, blk, meta_ref, xs_ref, rows):
    if case == "first":
        return jnp.concatenate([meta_ref[...], blk[0:rows.tmh - N_META]], axis=0)
    if case == "mid":
        return blk
    n_prompt = rows.lr - (rows.r - rows.tmh)
    parts = [blk[rows.tmh - n_prompt:rows.tmh]]
    for n, val in ((rows.s0 - rows.lr, None), (rows.ns, xs_ref), (rows.r - rows.s0 - rows.ns, None)):
        if n:
            parts.append(jnp.zeros((n, blk.shape[1]), F32) if val is None else val[...])
    return jnp.concatenate(parts, axis=0)


def _per_tile_case(i, n, body):
    pl.when(i == 0)(functools.partial(body, "first"))
    pl.when((i > 0) & (i < n - 1))(functools.partial(body, "mid"))
    pl.when(i == n - 1)(functools.partial(body, "last"))


def _gate_kernel(x_ref, meta_ref, xs_ref, g_ref, b_ref, wf_ref, bf_ref, lanej_ref,
                 hb_ref, qb_ref, kb_ref, lf_ref, carry_scr, *, rows):
    i = pl.program_id(0)
    tm = rows.tmh

    @pl.when(i == 0)
    def _():
        carry_scr[...] = jnp.zeros_like(carry_scr)

    def body(case):
        x = _row_space_x(case, x_ref[...], meta_ref, xs_ref, rows)
        hb = _layer_norm(x, g_ref[...], b_ref[...]).astype(BF16)
        hb_ref[...] = hb
        ff = jnp.dot(hb, wf_ref[...], preferred_element_type=F32) + bf_ref[...]
        lf = jnp.minimum(ff, 0.0) - jnp.log1p(jnp.exp(-jnp.abs(ff)))
        lf_ref[...] = lf[:, 0:N_HEADS]
        row = lax.broadcasted_iota(jnp.int32, (tm, tm), 0)
        col = lax.broadcasted_iota(jnp.int32, (tm, tm), 1)
        tri = (col <= row).astype(F32).astype(BF16)
        pieces = jnp.concatenate(_split3(lf), axis=1).astype(BF16)
        c3 = jnp.dot(tri, pieces, preferred_element_type=F32)
        cs = c3[:, 0:LANE] + c3[:, LANE:2 * LANE] + c3[:, 2 * LANE:3 * LANE] + carry_scr[0:1, :]
        carry_scr[...] = jnp.broadcast_to(cs[tm - 1:tm, :], carry_scr.shape)
        hi, mid, lo = _split3(cs)
        j = jnp.broadcast_to(lanej_ref[...], (tm, LANE))
        one = jnp.ones((tm, LANE), F32)
        zero = jnp.zeros((tm, LANE), F32)
        qb_ref[...] = jnp.where(j == 0, hi, jnp.where(j == 1, mid, jnp.where(
            j == 2, lo, jnp.where(j >= 3, one, zero)))).astype(BF16)
        kb_ref[...] = jnp.where(j == 3, -hi, jnp.where(j == 4, -mid, jnp.where(
            j == 5, -lo, jnp.where(j >= 0, one, zero)))).astype(BF16)

    _per_tile_case(i, rows.n_half, body)


def _ln_gate(x_prompt, meta, x_sample, ln_g, ln_b, w_f, b_f, lane_j, rows):
    D = x_prompt.shape[1]
    tm = rows.tmh
    row = lambda i: (i, 0)
    const = lambda i: (0, 0)
    return pl.pallas_call(
        functools.partial(_gate_kernel, rows=rows),
        grid=(rows.n_half,),
        in_specs=[_x_block_spec(rows, D), pl.BlockSpec(meta.shape, const), pl.BlockSpec(x_sample.shape, const),
                  pl.BlockSpec((1, D), const), pl.BlockSpec((1, D), const),
                  pl.BlockSpec((D, LANE), const), pl.BlockSpec((1, LANE), const), pl.BlockSpec((1, LANE), const)],
        out_specs=[pl.BlockSpec((tm, D), row), pl.BlockSpec((tm, LANE), row),
                   pl.BlockSpec((tm, LANE), row), pl.BlockSpec((tm, N_HEADS), row)],
        out_shape=[jax.ShapeDtypeStruct((rows.r, D), BF16),
                   jax.ShapeDtypeStruct((rows.r, LANE), BF16),
                   jax.ShapeDtypeStruct((rows.r, LANE), BF16),
                   jax.ShapeDtypeStruct((rows.r, N_HEADS), F32)],
        scratch_shapes=[pltpu.VMEM((SUBLANE, LANE), F32)],
        compiler_params=pltpu.CompilerParams(dimension_semantics=("arbitrary",), vmem_limit_bytes=VMEM_LIMIT),
        name="ln_gate",
    )(x_prompt, meta, x_sample, ln_g, ln_b, w_f, b_f, lane_j)


def _rope(zh, cos, sa, sb):
    return zh * cos + pltpu.roll(zh, D_HEAD - ROT_DIM // 2, 1) * sa + pltpu.roll(zh, ROT_DIM // 2, 1) * sb


def _proj_kernel(hb_ref, wfox_ref, wdiff_ref, cos_ref, sa_ref, sb_ref, zb_ref, pk_ref, pv_ref, pdk_ref, pdv_ref,
                 sk_ref, sv_ref, sdk_ref, sdv_ref, *, tm, s_tile, s_off, ns):
    j = pl.program_id(0)
    i = pl.program_id(1)
    heads = [slice(h * D_HEAD, (h + 1) * D_HEAD) for h in range(N_HEADS)]

    def matmul(w_ref):
        return jnp.dot(hb_ref[...], w_ref[...], preferred_element_type=F32)

    def cache_store(per_head, p_ref, s_ref):
        for h in range(N_HEADS):
            p_ref[pl.ds(h, tm, stride=N_HEADS), :] = per_head[h]

        @pl.when(i == s_tile)
        def _():
            for h in range(N_HEADS):
                s_ref[pl.ds(h, ns, stride=N_HEADS), :] = per_head[h][s_off:s_off + ns]

    @pl.when(j == 0)
    def _():
        zb_ref[...] = (matmul(wfox_ref) * FOX_SCALE).astype(BF16)

    def plain(w_ref, p_ref, s_ref):
        z = matmul(w_ref)
        zb_ref[...] = z.astype(BF16)
        cache_store([z[:, hs] for hs in heads], p_ref, s_ref)

    pl.when(j == 1)(functools.partial(plain, wfox_ref, pk_ref, sk_ref))
    pl.when(j == 2)(functools.partial(plain, wfox_ref, pv_ref, sv_ref))
    pl.when(j == 5)(functools.partial(plain, wdiff_ref, pdv_ref, sdv_ref))

    @pl.when(j == 3)
    def _():
        z = matmul(wdiff_ref)
        cos, sa, sb = cos_ref[...], sa_ref[...], sb_ref[...]
        for hs in heads:
            zb_ref[:, hs] = (_rope(z[:, hs], cos, sa, sb) * DIFF_SCALE).astype(BF16)

    @pl.when(j == 4)
    def _():
        z = matmul(wdiff_ref)
        cos, sa, sb = cos_ref[...], sa_ref[...], sb_ref[...]
        ys = [_rope(z[:, hs], cos, sa, sb) for hs in heads]
        for hs, y in zip(heads, ys):
            zb_ref[:, hs] = y.astype(BF16)
        cache_store(ys, pdk_ref, sdk_ref)


def _project(hb, w_fox, w_diff, cos_t, sa_t, sb_t, tm, n_prompt_rows, s0, ns):
    R, D = hb.shape
    n = R // tm
    half = N_GROUPS // 2
    row = lambda j, i: (i, 0)

    def cache_spec(group):
        return pl.BlockSpec((tm * N_HEADS, D_HEAD),
                            lambda j, i: (jnp.where(j < group, 0, jnp.where(j == group, i, n - 1)), 0))

    sample_spec = pl.BlockSpec((ns * N_HEADS, D_HEAD), lambda j, i: (0, 0))
    cache_shape = jax.ShapeDtypeStruct((n_prompt_rows * N_HEADS, D_HEAD), F32)
    sample_shape = jax.ShapeDtypeStruct((ns * N_HEADS, D_HEAD), F32)
    return pl.pallas_call(
        functools.partial(_proj_kernel, tm=tm, s_tile=s0 // tm, s_off=s0 % tm, ns=ns),
        grid=(N_GROUPS, n),
        in_specs=[pl.BlockSpec((tm, D), row),
                  pl.BlockSpec((D, GROUP), lambda j, i: (0, jnp.minimum(j, half - 1))),
                  pl.BlockSpec((D, GROUP), lambda j, i: (0, jnp.maximum(j - half, 0))),
                  pl.BlockSpec((tm, LANE), row), pl.BlockSpec((tm, LANE), row), pl.BlockSpec((tm, LANE), row)],
        out_specs=[pl.BlockSpec((tm, GROUP), lambda j, i: (i, j)),
                   cache_spec(1), cache_spec(2), cache_spec(4), cache_spec(5),
                   sample_spec, sample_spec, sample_spec, sample_spec],
        out_shape=[jax.ShapeDtypeStruct((R, N_GROUPS * GROUP), BF16),
                   cache_shape, cache_shape, cache_shape, cache_shape,
                   sample_shape, sample_shape, sample_shape, sample_shape],
        compiler_params=pltpu.CompilerParams(
            dimension_semantics=("arbitrary", "arbitrary"), vmem_limit_bytes=VMEM_LIMIT),
        name="in_proj",
    )(hb, w_fox, w_diff, cos_t, sa_t, sb_t)


def _flash_block(qs_scr, k_ref, kb_ref, v_ref, m_scr, acc_scr, h, r0, nr, nc, keep):
    hs = slice(h * D_HEAD, (h + 1) * D_HEAD)
    kh = k_ref[0:nc, hs]
    if kb_ref is not None:
        kh = jnp.concatenate([kh, kb_ref[0:nc, :]], axis=1)
    s = lax.dot_general(qs_scr[h, r0:r0 + nr, :], kh, NT_DIMS, preferred_element_type=F32)
    if keep is not None:
        s = jnp.where(keep, s, NEG_INF)
    m_prev = m_scr[h, r0:r0 + nr, :]
    m_new = jnp.maximum(m_prev, jnp.max(s, axis=1, keepdims=True))
    p = jnp.exp(s - _rep(m_new, nc // LANE))
    alpha = jnp.exp(m_prev - m_new)
    vh = jnp.concatenate([v_ref[0:nc, hs], jnp.ones((nc, D_HEAD), BF16)], axis=1)
    pv = jnp.dot(p.astype(BF16), vh, preferred_element_type=F32)
    acc_scr[h, r0:r0 + nr, :] = acc_scr[h, r0:r0 + nr, :] * _rep(alpha, 2) + pv
    m_scr[h, r0:r0 + nr, :] = m_new


def _flash_full(qs_scr, k_ref, kb_ref, v_ref, m_scr, acc_scr, *, tq, nmap):
    for h in range(N_HEADS):
        _flash_block(qs_scr, k_ref, kb_ref, v_ref, m_scr, acc_scr, h, 0, nmap * tq, tq, None)


def _flash_diag(qs_scr, k_ref, kb_ref, v_ref, m_scr, acc_scr, *, tq, nmap):
    sub = tq // DIAG_BLOCKS
    for h in range(N_HEADS):
        for rb in range(DIAG_BLOCKS):
            nc = (rb + 1) * sub
            row = lax.broadcasted_iota(jnp.int32, (sub, nc), 0) + rb * sub
            keep = lax.broadcasted_iota(jnp.int32, (sub, nc), 1) <= row
            for mp in range(nmap):
                _flash_block(qs_scr, k_ref, kb_ref, v_ref, m_scr, acc_scr, h, mp * tq + rb * sub, sub, nc, keep)


def _flash_init(m_scr, acc_scr):
    m_scr[...] = jnp.full(m_scr.shape, NEG_INF, F32)
    acc_scr[...] = jnp.zeros(acc_scr.shape, F32)


def _fox_kernel(qi_tab, ki_tab, q_ref, k_ref, v_ref, qb_ref, kb_ref, o_ref, qs_scr, m_scr, acc_scr, *, tq):
    s = pl.program_id(0)
    qi = qi_tab[s]
    ki = ki_tab[s]

    @pl.when(ki == 0)
    def _():
        _flash_init(m_scr, acc_scr)
        qb = qb_ref[...].astype(F32)
        lane = lax.broadcasted_iota(jnp.int32, qb.shape, 1)
        for h in range(N_HEADS):
            lo = BIAS_BASE + BIAS_W * h
            qbh = jnp.where((lane >= lo) & (lane < lo + BIAS_W), qb, 0.0).astype(BF16)
            qs_scr[h] = jnp.concatenate([q_ref[:, h * D_HEAD:(h + 1) * D_HEAD], qbh], axis=1)

    args = (qs_scr, k_ref, kb_ref, v_ref, m_scr, acc_scr)

    @pl.when(ki != qi)
    def _():
        _flash_full(*args, tq=tq, nmap=1)

    @pl.when(ki == qi)
    def _():
        _flash_diag(*args, tq=tq, nmap=1)
        for h in range(N_HEADS):
            a = acc_scr[h]
            o_ref[:, h * D_HEAD:(h + 1) * D_HEAD] = (a[:, :D_HEAD] / a[:, D_HEAD:]).astype(o_ref.dtype)


def _diff_lambda(lamp_ref):
    lp = lamp_ref[...]
    return (jnp.exp(jnp.sum(lp[0:1] * lp[1:2], axis=1, keepdims=True))
            - jnp.exp(jnp.sum(lp[2:3] * lp[3:4], axis=1, keepdims=True)) + LAM_INIT)


def _diff_finish(o1, o2, lam, gn):
    od = o1 - lam * o2
    od = od * lax.rsqrt(jnp.mean(od * od, axis=-1, keepdims=True) + LN_EPS)
    return od * gn * (1.0 - LAM_INIT)


def _diff_kernel(qi_tab, ki_tab, q_ref, k_ref, v_ref, lamp_ref, gn_ref, o_ref, qs_scr, m_scr, acc_scr, *, tq):
    s = pl.program_id(0)
    qi = qi_tab[s]
    ki = ki_tab[s]

    @pl.when(ki == 0)
    def _():
        _flash_init(m_scr, acc_scr)
        lane = lax.broadcasted_iota(jnp.int32, (tq, D_HEAD), 1)
        for h in range(N_HEADS):
            qh = q_ref[:, h * D_HEAD:(h + 1) * D_HEAD].astype(F32)
            qs_scr[h, 0:tq, :] = jnp.where(lane < D_MAP, qh, 0.0).astype(BF16)
            qs_scr[h, tq:2 * tq, :] = jnp.where(lane >= D_MAP, qh, 0.0).astype(BF16)

    args = (qs_scr, k_ref, None, v_ref, m_scr, acc_scr)

    @pl.when(ki != qi)
    def _():
        _flash_full(*args, tq=tq, nmap=2)

    @pl.when(ki == qi)
    def _():
        _flash_diag(*args, tq=tq, nmap=2)
        lam = _diff_lambda(lamp_ref)
        gn = gn_ref[...]
        for h in range(N_HEADS):
            a = acc_scr[h]
            o1 = a[0:tq, :D_HEAD] / a[0:tq, D_HEAD:]
            o2 = a[tq:2 * tq, :D_HEAD] / a[tq:2 * tq, D_HEAD:]
            o_ref[:, h * D_HEAD:(h + 1) * D_HEAD] = _diff_finish(o1, o2, lam, gn).astype(o_ref.dtype)


def _causal_steps(n_tiles):
    qi = [q for q in range(n_tiles) for _ in range(q + 1)]
    ki = [k for q in range(n_tiles) for k in range(q + 1)]
    return jnp.asarray(qi, jnp.int32), jnp.asarray(ki, jnp.int32)


def _fox_attention(zb, qb, kb, tq):
    R = zb.shape[0]
    qi_tab, ki_tab = _causal_steps(R // tq)
    grid_spec = pltpu.PrefetchScalarGridSpec(
        num_scalar_prefetch=2,
        grid=(qi_tab.shape[0],),
        in_specs=[
            pl.BlockSpec((tq, GROUP), lambda s, qt, kt: (qt[s], 0)),
            pl.BlockSpec((tq, GROUP), lambda s, qt, kt: (kt[s], 1)),
            pl.BlockSpec((tq, GROUP), lambda s, qt, kt: (kt[s], 2)),
            pl.BlockSpec((tq, LANE), lambda s, qt, kt: (qt[s], 0)),
            pl.BlockSpec((tq, LANE), lambda s, qt, kt: (kt[s], 0)),
        ],
        out_specs=pl.BlockSpec((tq, GROUP), lambda s, qt, kt: (qt[s], 0)),
        scratch_shapes=[
            pltpu.VMEM((N_HEADS, tq, 2 * D_HEAD), BF16),
            pltpu.VMEM((N_HEADS, tq, LANE), F32),
            pltpu.VMEM((N_HEADS, tq, 2 * D_HEAD), F32),
        ],
    )
    return pl.pallas_call(
        functools.partial(_fox_kernel, tq=tq),
        grid_spec=grid_spec,
        out_shape=jax.ShapeDtypeStruct((R, GROUP), BF16),
        compiler_params=pltpu.CompilerParams(dimension_semantics=("arbitrary",), vmem_limit_bytes=VMEM_LIMIT),
        name="fox_flash",
    )(qi_tab, ki_tab, zb, zb, zb, qb, kb)


def _diff_attention(zb, lamp, gn, tq):
    R = zb.shape[0]
    qi_tab, ki_tab = _causal_steps(R // tq)
    grid_spec = pltpu.PrefetchScalarGridSpec(
        num_scalar_prefetch=2,
        grid=(qi_tab.shape[0],),
        in_specs=[
            pl.BlockSpec((tq, GROUP), lambda s, qt, kt: (qt[s], 3)),
            pl.BlockSpec((tq, GROUP), lambda s, qt, kt: (kt[s], 4)),
            pl.BlockSpec((tq, GROUP), lambda s, qt, kt: (kt[s], 5)),
            pl.BlockSpec((SUBLANE, LANE), lambda s, qt, kt: (0, 0)),
            pl.BlockSpec((1, LANE), lambda s, qt, kt: (0, 0)),
        ],
        out_specs=pl.BlockSpec((tq, GROUP), lambda s, qt, kt: (qt[s], 0)),
        scratch_shapes=[
            pltpu.VMEM((N_HEADS, 2 * tq, D_HEAD), BF16),
            pltpu.VMEM((N_HEADS, 2 * tq, LANE), F32),
            pltpu.VMEM((N_HEADS, 2 * tq, 2 * D_HEAD), F32),
        ],
    )
    return pl.pallas_call(
        functools.partial(_diff_kernel, tq=tq),
        grid_spec=grid_spec,
        out_shape=jax.ShapeDtypeStruct((R, GROUP), BF16),
        compiler_params=pltpu.CompilerParams(dimension_semantics=("arbitrary",), vmem_limit_bytes=VMEM_LIMIT),
        name="diff_flash",
    )(qi_tab, ki_tab, zb, zb, zb, lamp, gn)


def _decode_kernel(pt_ref, *refs, pages_per_step):
    P = pages_per_step
    (qf_ref, kf_ref, vf_ref, lfo_ref, qd_ref, kd_ref, vd_ref, ux_ref, lamp_ref, gn_ref) = refs[:10]
    kc = refs[10:10 + P]
    vc = refs[10 + P:10 + 2 * P]
    lc = refs[10 + 2 * P:10 + 3 * P]
    dkc = refs[10 + 3 * P:10 + 4 * P]
    dvc = refs[10 + 4 * P:10 + 5 * P]
    of_ref, od_ref = refs[10 + 5 * P:12 + 5 * P]
    mf, lf, af, cf, qf16, md, ld, ad, qd16 = refs[12 + 5 * P:]
    j = pl.program_id(1)
    page_rows = kc[0].shape[0]
    n_rep = page_rows // LANE
    dup = lambda x: jnp.concatenate([x, x], axis=0)

    @pl.when(j == 0)
    def _():
        q2 = dup(qf_ref[0])
        qf16[...] = q2.astype(BF16)
        s_own = jnp.sum(q2 * dup(kf_ref[0]), axis=1, keepdims=True)
        mf[...] = jnp.broadcast_to(s_own, mf.shape)
        lf[...] = jnp.ones(lf.shape, F32)
        af[...] = dup(vf_ref[0])
        cf[...] = dup(lfo_ref[0])
        lane = lax.broadcasted_iota(jnp.int32, (N_HEADS, D_HEAD), 1)
        qd = qd_ref[0]
        q16 = jnp.concatenate([jnp.where(lane < D_MAP, qd, 0.0), jnp.where(lane >= D_MAP, qd, 0.0)], axis=0)
        qd16[...] = q16.astype(BF16)
        sd_own = jnp.sum(q16 * dup(kd_ref[0]), axis=1, keepdims=True)
        md[...] = jnp.broadcast_to(sd_own, md.shape)
        ld[...] = jnp.ones(ld.shape, F32)
        ad[...] = dup(vd_ref[0])

    lane8 = lax.broadcasted_iota(jnp.int32, (2 * N_HEADS, page_rows), 1) & (N_HEADS - 1)
    row8 = lax.broadcasted_iota(jnp.int32, (2 * N_HEADS, page_rows), 0) & (N_HEADS - 1)
    valid = lane8 == row8
    ux = ux_ref[...]

    def online(scores, m_ref, l_ref, a_ref, v_pages):
        m_prev = m_ref[...]
        m_new = m_prev
        for s in scores:
            m_new = jnp.maximum(m_new, jnp.max(s, axis=1, keepdims=True))
        alpha = jnp.exp(m_prev - m_new)
        m_rep = _rep(m_new, n_rep)
        l_new = alpha * l_ref[...]
        a_new = alpha * a_ref[...]
        for s, v_page in zip(scores, v_pages):
            p = jnp.exp(s - m_rep)
            l_new = l_new + jnp.sum(p, axis=1, keepdims=True)
            a_new = a_new + jnp.dot(p.astype(BF16), v_page[...].astype(BF16), preferred_element_type=F32)
        l_ref[...] = l_new
        a_ref[...] = a_new
        m_ref[...] = m_new

    carry = cf[...]
    fox_scores = []
    for r in range(P):
        s_all = lax.dot_general(qf16[...], kc[r][...].astype(BF16), NT_DIMS, preferred_element_type=F32)
        lt = lc[r][0]
        pieces = jnp.concatenate(_split3(lt), axis=0).astype(BF16)
        b3 = jnp.dot(pieces, ux, preferred_element_type=F32)
        bias = dup(b3[0:N_HEADS] + b3[N_HEADS:2 * N_HEADS] + b3[2 * N_HEADS:3 * N_HEADS]) + _rep(carry, n_rep)
        fox_scores.append(jnp.where(valid, s_all + bias, NEG_INF))
        carry = carry + dup(jnp.sum(lt, axis=1, keepdims=True))
    cf[...] = carry
    online(fox_scores, mf, lf, af, vc)

    diff_scores = [jnp.where(valid, lax.dot_general(qd16[...], dkc[r][...].astype(BF16), NT_DIMS,
                                                    preferred_element_type=F32), NEG_INF) for r in range(P)]
    online(diff_scores, md, ld, ad, dvc)

    @pl.when(j == pl.num_programs(1) - 1)
    def _():
        of_ref[0] = (af[...] / lf[...])[0:N_HEADS]
        o = ad[...] / ld[...]
        od_ref[0] = _diff_finish(o[0:N_HEADS], o[N_HEADS:], _diff_lambda(lamp_ref), gn_ref[...])


def _decode_attention(page_table, small, ux, lamp, gn, caches, lf_cache_t, pages_per_step):
    NS, n_pages = page_table.shape
    P = pages_per_step
    page_rows = caches[0].shape[0] // lf_cache_t.shape[0]
    per_seq = pl.BlockSpec((1, N_HEADS, D_HEAD), lambda b, j, pt: (b, 0, 0))

    def page_spec(r):
        return pl.BlockSpec((page_rows, D_HEAD), lambda b, j, pt: (pt[b, n_pages - 1 - (j * P + r)], 0))

    def lf_spec(r):
        return pl.BlockSpec((1, N_HEADS, LANE), lambda b, j, pt: (pt[b, n_pages - 1 - (j * P + r)], 0, 0))

    kc, vc, dkc, dvc = caches
    in_specs = ([per_seq] * 7
                + [pl.BlockSpec(ux.shape, lambda b, j, pt: (0, 0)),
                   pl.BlockSpec((SUBLANE, LANE), lambda b, j, pt: (0, 0)),
                   pl.BlockSpec((1, LANE), lambda b, j, pt: (0, 0))]
                + [page_spec(r) for r in range(P)] * 2
                + [lf_spec(r) for r in range(P)]
                + [page_spec(r) for r in range(P)] * 2)
    args = (list(small) + [ux, lamp, gn] + [kc] * P + [vc] * P + [lf_cache_t] * P + [dkc] * P + [dvc] * P)
    grid_spec = pltpu.PrefetchScalarGridSpec(
        num_scalar_prefetch=1,
        grid=(NS, n_pages // P),
        in_specs=in_specs,
        out_specs=[per_seq, per_seq],
        scratch_shapes=(
            [pltpu.VMEM((2 * N_HEADS, LANE), F32)] * 4 + [pltpu.VMEM((2 * N_HEADS, D_HEAD), BF16)]
            + [pltpu.VMEM((2 * N_HEADS, LANE), F32)] * 3 + [pltpu.VMEM((2 * N_HEADS, D_HEAD), BF16)]),
    )
    return pl.pallas_call(
        functools.partial(_decode_kernel, pages_per_step=P),
        grid_spec=grid_spec,
        out_shape=[jax.ShapeDtypeStruct((NS, N_HEADS, D_HEAD), F32)] * 2,
        compiler_params=pltpu.CompilerParams(
            dimension_semantics=("arbitrary", "arbitrary"), vmem_limit_bytes=VMEM_LIMIT),
        name="paged_decode",
    )(page_table, *args)


def _mix_kernel(x_ref, meta_ref, xs_ref, of_ref, od_ref, sf_ref, sd_ref, wt_ref, wb_ref, gi_ref, bi_ref,
                g_ref, b_ref, o_ref, os_ref, *, rows):
    def mixed(x, o_f, o_d):
        y = (ALPHA * _layer_norm(x, gi_ref[...], bi_ref[...])
             + jnp.dot(o_f, wt_ref[...], preferred_element_type=F32)
             + jnp.dot(o_d, wb_ref[...], preferred_element_type=F32))
        return _layer_norm(y, g_ref[...], b_ref[...])

    def body(case):
        o_ref[...] = mixed(_row_space_x(case, x_ref[...], meta_ref, xs_ref, rows), of_ref[...], od_ref[...])
        if case == "last":
            os_ref[...] = mixed(xs_ref[...], sf_ref[...], sd_ref[...])

    _per_tile_case(pl.program_id(0), rows.n_half, body)


def _mix_out(x_prompt, meta, x_sample, o_f, o_d, os_f, os_d, wo_top, wo_bot, gi, bi, g, b, rows):
    D = x_prompt.shape[1]
    tm = rows.tmh
    row = lambda i: (i, 0)
    const = lambda i: (0, 0)
    return pl.pallas_call(
        functools.partial(_mix_kernel, rows=rows),
        grid=(rows.n_half,),
        in_specs=[_x_block_spec(rows, D), pl.BlockSpec(meta.shape, const), pl.BlockSpec(x_sample.shape, const),
                  pl.BlockSpec((tm, GROUP), row), pl.BlockSpec((tm, GROUP), row),
                  pl.BlockSpec((rows.ns, GROUP), const), pl.BlockSpec((rows.ns, GROUP), const),
                  pl.BlockSpec((GROUP, D), const), pl.BlockSpec((GROUP, D), const),
                  pl.BlockSpec((1, D), const), pl.BlockSpec((1, D), const),
                  pl.BlockSpec((1, D), const), pl.BlockSpec((1, D), const)],
        out_specs=[pl.BlockSpec((tm, D), row), pl.BlockSpec((rows.ns, D), const)],
        out_shape=[jax.ShapeDtypeStruct((rows.r, D), F32),
                   jax.ShapeDtypeStruct((rows.ns, D), F32)],
        compiler_params=pltpu.CompilerParams(dimension_semantics=("arbitrary",), vmem_limit_bytes=VMEM_LIMIT),
        name="mix_out_ln1",
    )(x_prompt, meta, x_sample, o_f, o_d, os_f, os_d, wo_top, wo_bot, gi, bi, g, b)


def _ffn_kernel(h_ref, wu_ref, wd_ref, g_ref, b_ref, o_ref, hb_scr, acc_scr):
    j = pl.program_id(1)

    @pl.when(j == 0)
    def _():
        hb_scr[...] = h_ref[...].astype(BF16)
        acc_scr[...] = jnp.zeros_like(acc_scr)

    u = jnp.maximum(jnp.dot(hb_scr[...], wu_ref[...], preferred_element_type=F32), 0.0)
    acc_scr[...] += jnp.dot((u * u).astype(BF16), wd_ref[...], preferred_element_type=F32)

    @pl.when(j == pl.num_programs(1) - 1)
    def _():
        o_ref[...] = _layer_norm(ALPHA * h_ref[...] + acc_scr[...], g_ref[...], b_ref[...])


def _ffn(h1, w_up, w_down, g, b, n_rows, row0, tm, tf):
    D = h1.shape[1]
    d_ff = w_up.shape[1]
    return pl.pallas_call(
        _ffn_kernel,
        grid=(n_rows // tm, d_ff // tf),
        in_specs=[pl.BlockSpec((pl.Element(tm), pl.Element(D)),
                               lambda i, j: (pl.multiple_of(row0 + i * tm, SUBLANE), 0)),
                  pl.BlockSpec((D, tf), lambda i, j: (0, j)),
                  pl.BlockSpec((tf, D), lambda i, j: (j, 0)),
                  pl.BlockSpec((1, D), lambda i, j: (0, 0)),
                  pl.BlockSpec((1, D), lambda i, j: (0, 0))],
        out_specs=pl.BlockSpec((tm, D), lambda i, j: (i, 0)),
        out_shape=jax.ShapeDtypeStruct((n_rows, D), F32),
        scratch_shapes=[pltpu.VMEM((tm, D), BF16), pltpu.VMEM((tm, D), F32)],
        compiler_params=pltpu.CompilerParams(
            dimension_semantics=("arbitrary", "arbitrary"), vmem_limit_bytes=VMEM_LIMIT),
        name="ffn_ln2",
    )(h1, w_up, w_down, g, b)


def _rope_tables(pos):
    half = ROT_DIM // 2
    inv_freq = jnp.power(jnp.float32(ROPE_THETA), -jnp.arange(0, ROT_DIM, 2, dtype=F32) / ROT_DIM)
    ang = pos.astype(F32)[:, None] * inv_freq[None, :]
    cos, sin = jnp.cos(ang), jnp.sin(ang)
    lane = jnp.arange(D_HEAD) % D_MAP
    f = lane % half
    first, second = lane < half, (lane >= half) & (lane < ROT_DIM)
    cos_t = jnp.where((first | second)[None, :], cos[:, f], 1.0)
    sa_t = jnp.where(first[None, :], -sin[:, f], 0.0)
    sb_t = jnp.where(second[None, :], sin[:, f], 0.0)
    return cos_t, sa_t, sb_t


def kernel(x_prompt, x_sample, cache_fox_k, cache_fox_v, cache_fox_logf, cache_diff_k, cache_diff_v, page_table, meta_tokens, ln_in_g, ln_in_b, w_in, b_forget, lambda_q1, lambda_k1, lambda_q2, lambda_k2, diff_norm_g, w_o, ln1_g, ln1_b, w_up, w_down, ln2_g, ln2_b):
    batch, seq, D = x_prompt.shape
    NS, dec_seq, _ = x_sample.shape
    depth, n_pool, page_size = cache_fox_k.shape[:3]
    n_pages = page_table.shape[1]
    assert batch == 1 and dec_seq == 1 and depth == DEPTH
    assert cache_fox_k.shape[3:] == (N_HEADS, D_HEAD) and meta_tokens.shape[0] == N_META
    past_len = n_pages * page_size

    tm = ROW_TILE
    Lr = N_META + seq
    S0 = _round_up(Lr, 32)
    R = _round_up(S0 + NS, tm)
    rows = _Rows(seq=seq, ns=NS, lr=Lr, s0=S0, r=R, tmh=tm // 2)
    assert rows.n_half >= 2 and R - rows.tmh <= Lr and S0 >= R - rows.tmh and seq >= rows.tmh
    xp, xs, meta = x_prompt[0], x_sample[:, 0], meta_tokens.astype(F32)
    pos = jnp.concatenate([jnp.arange(Lr), jnp.zeros((S0 - Lr,), jnp.int32),
                           jnp.full((NS,), past_len), jnp.zeros((R - S0 - NS,), jnp.int32)])
    cos_t, sa_t, sb_t = _rope_tables(pos)

    w_fox = w_in[0][:, :N_GROUPS // 2 * GROUP].astype(BF16)
    w_diff = w_in[0][:, N_GROUPS // 2 * GROUP + N_HEADS:].astype(BF16)
    wf = w_in[0][:, N_GROUPS // 2 * GROUP:N_GROUPS // 2 * GROUP + N_HEADS]
    lane = np.arange(LANE)
    bias_lane = (lane >= BIAS_BASE) & (lane < BIAS_BASE + BIAS_W * N_HEADS)
    head_of_lane = np.where(bias_lane, (lane - BIAS_BASE) // BIAS_W, lane % N_HEADS)
    used = bias_lane | (lane < N_HEADS)
    w_f = jnp.where(used[None, :], wf[:, head_of_lane], 0.0).astype(BF16)
    b_f = jnp.where(used, b_forget[0].astype(F32)[head_of_lane], 0.0)[None, :]
    lane_j = jnp.asarray(np.where(bias_lane, (lane - BIAS_BASE) % BIAS_W, -1)[None, :], jnp.int32)
    row2 = lambda v: v.astype(F32).reshape(1, -1)
    gi, bi = row2(ln_in_g), row2(ln_in_b)

    hb, qb, kb, lf = _ln_gate(xp, meta, xs, gi, bi, w_f, b_f, lane_j, rows)
    zb, p_fk, p_fv, p_dk, p_dv, s_fk, s_fv, s_dk, s_dv = _project(hb, w_fox, w_diff, cos_t, sa_t, sb_t,
                                                                  tm, Lr, S0, NS)

    lamp = jnp.zeros((SUBLANE, LANE), F32).at[0:4, 0:D_MAP].set(
        jnp.stack([lambda_q1[0], lambda_k1[0], lambda_q2[0], lambda_k2[0]]).astype(F32))
    gn = row2(diff_norm_g[0])
    o_f = _fox_attention(zb, qb, kb, tm)
    o_d = _diff_attention(zb, lamp, gn, tm)

    grp = lambda g: zb[S0:S0 + NS, g * GROUP:(g + 1) * GROUP].reshape(NS, N_HEADS, D_HEAD).astype(F32)
    lf_own = jnp.broadcast_to(lf[S0:S0 + NS, :, None], (NS, N_HEADS, LANE))
    small = (grp(0), grp(1), grp(2), lf_own, grp(3), grp(4), grp(5))
    page_rows = page_size * N_HEADS
    flat = lambda c: c[0].reshape(n_pool * page_rows, D_HEAD)
    lf_cache_t = jnp.swapaxes(cache_fox_logf[0], 1, 2)
    s_idx = jnp.arange(page_size)
    ux = (s_idx[:, None] > jnp.repeat(s_idx, N_HEADS)[None, :]).astype(BF16)
    P = next(p for p in (8, 4, 2, 1) if n_pages % p == 0)
    os_f, os_d = _decode_attention(page_table, small, ux, lamp, gn,
                                   (flat(cache_fox_k), flat(cache_fox_v), flat(cache_diff_k), flat(cache_diff_v)),
                                   lf_cache_t, P)

    wo = w_o[0].astype(BF16)
    h1, h1_s = _mix_out(xp, meta, xs, o_f, o_d, os_f.reshape(NS, GROUP).astype(BF16),
                        os_d.reshape(NS, GROUP).astype(BF16), wo[:GROUP], wo[GROUP:], gi, bi,
                        row2(ln1_g[0]), row2(ln1_b[0]), rows)
    ffn = functools.partial(_ffn, w_up=w_up[0].astype(BF16), w_down=w_down[0].astype(BF16),
                            g=row2(ln2_g[0]), b=row2(ln2_b[0]), tf=1024)
    tm_ffn = next(t for t in (512, 256, 128) if seq % t == 0)
    y_prompt = ffn(h1, n_rows=seq, row0=N_META, tm=tm_ffn)
    y_sample = ffn(h1_s, n_rows=NS, row0=0, tm=NS)

    prompt_cache = lambda a: a.reshape(1, 1, Lr, N_HEADS, D_HEAD)
    sample_cache = lambda a: a.reshape(1, NS, 1, N_HEADS, D_HEAD)
    return (y_prompt[None], y_sample[:, None],
            prompt_cache(p_fk), prompt_cache(p_fv), lf[:Lr][None, None], prompt_cache(p_dk), prompt_cache(p_dv),
            sample_cache(s_fk), sample_cache(s_fv), lf[S0:S0 + NS][None, :, None],
            sample_cache(s_dk), sample_cache(s_dv))
```

```python
import functools
import math
from typing import NamedTuple

import jax
import jax.numpy as jnp
import numpy as np
from jax import lax
from jax.experimental import pallas as pl
from jax.experimental.pallas import tpu as pltpu

F32 = jnp.float32
BF16 = jnp.bfloat16

N_META = 16
N_HEADS = 8
D_HEAD = 128
D_MAP = 64
ROT_DIM = D_MAP // 4
ROPE_THETA = 500000.0
DEPTH = 1
ALPHA = (2 * DEPTH) ** 0.25
LN_EPS = 1e-5
NEG_INF = -1e30
LAM_INIT = 0.8 - 0.6 * math.exp(-0.3 * 0)
FOX_SCALE = D_HEAD ** -0.5
DIFF_SCALE = D_MAP ** -0.5
GROUP = N_HEADS * D_HEAD
N_GROUPS = 6
BIAS_BASE = N_HEADS
BIAS_W = 6

LANE = 128
SUBLANE = 8
MXU_DEPTH = 256
VMEM_LIMIT = 56 * 1024 * 1024
HEAD_GROUPS = 2
HEADS_PER_GROUP = N_HEADS // HEAD_GROUPS
GROUP_W = HEADS_PER_GROUP * D_HEAD
DIAG_BLOCKS = 3
ROW_TILE = DIAG_BLOCKS * MXU_DEPTH

NT_DIMS = (((1,), (1,)), ((), ()))


def _round_up(x, m):
    return (x + m - 1) // m * m


def _layer_norm(x, g, b):
    mu = jnp.mean(x, -1, keepdims=True)
    xc = x - mu
    var = jnp.mean(xc * xc, -1, keepdims=True)
    return xc * lax.rsqrt(var + LN_EPS) * g + b


def _split3(x):
    hi = x.astype(BF16).astype(F32)
    r1 = x - hi
    mid = r1.astype(BF16).astype(F32)
    lo = (r1 - mid).astype(BF16).astype(F32)
    return hi, mid, lo


def _rep(x, n):
    return jnp.concatenate([x] * n, axis=1)


class _Rows(NamedTuple):
    seq: int
    ns: int
    lr: int
    s0: int
    r: int
    tmh: int

    @property
    def n_half(self):
        return self.r // self.tmh


def _x_block_spec(rows, d):
    def index(i, *_):
        return pl.multiple_of(jnp.clip(i * rows.tmh - N_META, 0, rows.seq - rows.tmh), SUBLANE), 0
    return pl.BlockSpec((pl.Element(rows.tmh), pl.Element(d)), index)


def _row_space_x(case, blk, meta_ref, xs_ref, rows):
    if case == "first":
        return jnp.concatenate([meta_ref[...], blk[0:rows.tmh - N_META]], axis=0)
    if case == "mid":
        return blk
    n_prompt = rows.lr - (rows.r - rows.tmh)
    parts = [blk[rows.tmh - n_prompt:rows.tmh]]
    for n, val in ((rows.s0 - rows.lr, None), (rows.ns, xs_ref), (rows.r - rows.s0 - rows.ns, None)):
        if n:
            parts.append(jnp.zeros((n, blk.shape[1]), F32) if val is None else val[...])
    return jnp.concatenate(parts, axis=0)


def _per_tile_case(i, n, body):
    pl.when(i == 0)(functools.partial(body, "first"))
    pl.when((i > 0) & (i < n - 1))(functools.partial(body, "mid"))
    pl.when(i == n - 1)(functools.partial(body, "last"))


def _gate_kernel(x_ref, meta_ref, xs_ref, g_ref, b_ref, wf_ref, bf_ref, lanej_ref,
                 hb_ref, qb_ref, kb_ref, lf_ref, carry_scr, *, rows):
    i = pl.program_id(0)
    tm = rows.tmh

    @pl.when(i == 0)
    def _():
        carry_scr[...] = jnp.zeros_like(carry_scr)

    def body(case):
        x = _row_space_x(case, x_ref[...], meta_ref, xs_ref, rows)
        hb = _layer_norm(x, g_ref[...], b_ref[...]).astype(BF16)
        hb_ref[...] = hb
        ff = jnp.dot(hb, wf_ref[...], preferred_element_type=F32) + bf_ref[...]
        lf = jnp.minimum(ff, 0.0) - jnp.log1p(jnp.exp(-jnp.abs(ff)))
        lf_ref[...] = lf[:, 0:N_HEADS]
        row = lax.broadcasted_iota(jnp.int32, (tm, tm), 0)
        col = lax.broadcasted_iota(jnp.int32, (tm, tm), 1)
        tri = (col <= row).astype(F32).astype(BF16)
        pieces = jnp.concatenate(_split3(lf), axis=1).astype(BF16)
        c3 = jnp.dot(tri, pieces, preferred_element_type=F32)
        cs = c3[:, 0:LANE] + c3[:, LANE:2 * LANE] + c3[:, 2 * LANE:3 * LANE] + carry_scr[0:1, :]
        carry_scr[...] = jnp.broadcast_to(cs[tm - 1:tm, :], carry_scr.shape)
        hi, mid, lo = _split3(cs)
        j = jnp.broadcast_to(lanej_ref[...], (tm, LANE))
        one = jnp.ones((tm, LANE), F32)
        zero = jnp.zeros((tm, LANE), F32)
        qb_ref[...] = jnp.where(j == 0, hi, jnp.where(j == 1, mid, jnp.where(
            j == 2, lo, jnp.where(j >= 3, one, zero)))).astype(BF16)
        kb_ref[...] = jnp.where(j == 3, -hi, jnp.where(j == 4, -mid, jnp.where(
            j == 5, -lo, jnp.where(j >= 0, one, zero)))).astype(BF16)

    _per_tile_case(i, rows.n_half, body)


def _ln_gate(x_prompt, meta, x_sample, ln_g, ln_b, w_f, b_f, lane_j, rows):
    D = x_prompt.shape[1]
    tm = rows.tmh
    row = lambda i: (i, 0)
    const = lambda i: (0, 0)
    return pl.pallas_call(
        functools.partial(_gate_kernel, rows=rows),
        grid=(rows.n_half,),
        in_specs=[_x_block_spec(rows, D), pl.BlockSpec(meta.shape, const), pl.BlockSpec(x_sample.shape, const),
                  pl.BlockSpec((1, D), const), pl.BlockSpec((1, D), const),
                  pl.BlockSpec((D, LANE), const), pl.BlockSpec((1, LANE), const), pl.BlockSpec((1, LANE), const)],
        out_specs=[pl.BlockSpec((tm, D), row), pl.BlockSpec((tm, LANE), row),
                   pl.BlockSpec((tm, LANE), row), pl.BlockSpec((tm, N_HEADS), row)],
        out_shape=[jax.ShapeDtypeStruct((rows.r, D), BF16),
                   jax.ShapeDtypeStruct((rows.r, LANE), BF16),
                   jax.ShapeDtypeStruct((rows.r, LANE), BF16),
                   jax.ShapeDtypeStruct((rows.r, N_HEADS), F32)],
        scratch_shapes=[pltpu.VMEM((SUBLANE, LANE), F32)],
        compiler_params=pltpu.CompilerParams(dimension_semantics=("arbitrary",), vmem_limit_bytes=VMEM_LIMIT),
        name="ln_gate",
    )(x_prompt, meta, x_sample, ln_g, ln_b, w_f, b_f, lane_j)


def _rope(zh, cos, sa, sb):
    return zh * cos + pltpu.roll(zh, D_HEAD - ROT_DIM // 2, 1) * sa + pltpu.roll(zh, ROT_DIM // 2, 1) * sb


def _proj_kernel(hb_ref, wfox_ref, wdiff_ref, cos_ref, sa_ref, sb_ref, zb_ref, pk_ref, pv_ref, pdk_ref, pdv_ref,
                 sk_ref, sv_ref, sdk_ref, sdv_ref, *, tm, s_tile, s_off, ns):
    j = pl.program_id(0)
    i = pl.program_id(1)
    heads = [slice(h * D_HEAD, (h + 1) * D_HEAD) for h in range(N_HEADS)]

    def matmul(w_ref):
        return jnp.dot(hb_ref[...], w_ref[...], preferred_element_type=F32)

    def cache_store(per_head, p_ref, s_ref):
        for h in range(N_HEADS):
            p_ref[pl.ds(h, tm, stride=N_HEADS), :] = per_head[h]

        @pl.when(i == s_tile)
        def _():
            for h in range(N_HEADS):
                s_ref[pl.ds(h, ns, stride=N_HEADS), :] = per_head[h][s_off:s_off + ns]

    @pl.when(j == 0)
    def _():
        zb_ref[...] = (matmul(wfox_ref) * FOX_SCALE).astype(BF16)

    def plain(w_ref, p_ref, s_ref):
        z = matmul(w_ref)
        zb_ref[...] = z.astype(BF16)
        cache_store([z[:, hs] for hs in heads], p_ref, s_ref)

    pl.when(j == 1)(functools.partial(plain, wfox_ref, pk_ref, sk_ref))
    pl.when(j == 2)(functools.partial(plain, wfox_ref, pv_ref, sv_ref))
    pl.when(j == 5)(functools.partial(plain, wdiff_ref, pdv_ref, sdv_ref))

    @pl.when(j == 3)
    def _():
        z = matmul(wdiff_ref)
        cos, sa, sb = cos_ref[...], sa_ref[...], sb_ref[...]
        for hs in heads:
            zb_ref[:, hs] = (_rope(z[:, hs], cos, sa, sb) * DIFF_SCALE).astype(BF16)

    @pl.when(j == 4)
    def _():
        z = matmul(wdiff_ref)
        cos, sa, sb = cos_ref[...], sa_ref[...], sb_ref[...]
        ys = [_rope(z[:, hs], cos, sa, sb) for hs in heads]
        for hs, y in zip(heads, ys):
            zb_ref[:, hs] = y.astype(BF16)
        cache_store(ys, pdk_ref, sdk_ref)


def _project(hb, w_fox, w_diff, cos_t, sa_t, sb_t, tm, n_prompt_rows, s0, ns):
    R, D = hb.shape
    n = R // tm
    half = N_GROUPS // 2
    row = lambda j, i: (i, 0)

    def cache_spec(group):
        return pl.BlockSpec((tm * N_HEADS, D_HEAD),
                            lambda j, i: (jnp.where(j < group, 0, jnp.where(j == group, i, n - 1)), 0))

    sample_spec = pl.BlockSpec((ns * N_HEADS, D_HEAD), lambda j, i: (0, 0))
    cache_shape = jax.ShapeDtypeStruct((n_prompt_rows * N_HEADS, D_HEAD), F32)
    sample_shape = jax.ShapeDtypeStruct((ns * N_HEADS, D_HEAD), F32)
    return pl.pallas_call(
        functools.partial(_proj_kernel, tm=tm, s_tile=s0 // tm, s_off=s0 % tm, ns=ns),
        grid=(N_GROUPS, n),
        in_specs=[pl.BlockSpec((tm, D), row),
                  pl.BlockSpec((D, GROUP), lambda j, i: (0, jnp.minimum(j, half - 1))),
                  pl.BlockSpec((D, GROUP), lambda j, i: (0, jnp.maximum(j - half, 0))),
                  pl.BlockSpec((tm, LANE), row), pl.BlockSpec((tm, LANE), row), pl.BlockSpec((tm, LANE), row)],
        out_specs=[pl.BlockSpec((tm, GROUP), lambda j, i: (i, j)),
                   cache_spec(1), cache_spec(2), cache_spec(4), cache_spec(5),
                   sample_spec, sample_spec, sample_spec, sample_spec],
        out_shape=[jax.ShapeDtypeStruct((R, N_GROUPS * GROUP), BF16),
                   cache_shape, cache_shape, cache_shape, cache_shape,
                   sample_shape, sample_shape, sample_shape, sample_shape],
        compiler_params=pltpu.CompilerParams(
            dimension_semantics=("arbitrary", "arbitrary"), vmem_limit_bytes=VMEM_LIMIT),
        name="in_proj",
    )(hb, w_fox, w_diff, cos_t, sa_t, sb_t)


def _flash_block(qs_scr, k_ref, kb_ref, v_ref, m_scr, acc_scr, h, r0, nr, nc, keep):
    hs = slice(h * D_HEAD, (h + 1) * D_HEAD)
    kh = k_ref[0:nc, hs]
    if kb_ref is not None:
        kh = jnp.concatenate([kh, kb_ref[0:nc, :]], axis=1)
    s = lax.dot_general(qs_scr[h, r0:r0 + nr, :], kh, NT_DIMS, preferred_element_type=F32)
    if keep is not None:
        s = jnp.where(keep, s, NEG_INF)
    m_prev = m_scr[h, r0:r0 + nr, :]
    m_new = jnp.maximum(m_prev, jnp.max(s, axis=1, keepdims=True))
    p = jnp.exp(s - _rep(m_new, nc // LANE))
    alpha = jnp.exp(m_prev - m_new)
    vh = jnp.concatenate([v_ref[0:nc, hs], jnp.ones((nc, D_HEAD), BF16)], axis=1)
    pv = jnp.dot(p.astype(BF16), vh, preferred_element_type=F32)
    acc_scr[h, r0:r0 + nr, :] = acc_scr[h, r0:r0 + nr, :] * _rep(alpha, 2) + pv
    m_scr[h, r0:r0 + nr, :] = m_new


def _flash_full(qs_scr, k_ref, kb_ref, v_ref, m_scr, acc_scr, *, tq, nmap):
    for h in range(HEADS_PER_GROUP):
        _flash_block(qs_scr, k_ref, kb_ref, v_ref, m_scr, acc_scr, h, 0, nmap * tq, tq, None)


def _flash_diag(qs_scr, k_ref, kb_ref, v_ref, m_scr, acc_scr, *, tq, nmap):
    sub = tq // DIAG_BLOCKS
    for h in range(HEADS_PER_GROUP):
        for rb in range(DIAG_BLOCKS):
            nc = (rb + 1) * sub
            row = lax.broadcasted_iota(jnp.int32, (sub, nc), 0) + rb * sub
            keep = lax.broadcasted_iota(jnp.int32, (sub, nc), 1) <= row
            for mp in range(nmap):
                _flash_block(qs_scr, k_ref, kb_ref, v_ref, m_scr, acc_scr, h, mp * tq + rb * sub, sub, nc, keep)


def _flash_init(m_scr, acc_scr):
    m_scr[...] = jnp.full(m_scr.shape, NEG_INF, F32)
    acc_scr[...] = jnp.zeros(acc_scr.shape, F32)


def _fox_kernel(qi_tab, ki_tab, pt_ref, q_ref, k_ref, v_ref, qb_ref, kb_ref,
                dq_ref, dk_ref, dv_ref, lamp_ref, gn_ref, *rest, tq, st):
    n = st.pps
    k_pages, v_pages = rest[:n], rest[n:2 * n]
    o_ref, os_ref, qs_scr, m_scr, acc_scr, q16, dm, dl, da = rest[2 * n:]
    g = pl.program_id(0)
    s = pl.program_id(1)
    qi = qi_tab[s]
    ki = ki_tab[s]
    _, _, c, active = _stream_pos(g, s, st)

    @pl.when(active & (c == 0))
    def _():
        _dec_init_diff(dq_ref, dk_ref, dv_ref, q16, dm, dl, da)

    @pl.when(ki == 0)
    def _():
        _flash_init(m_scr, acc_scr)
        qb = qb_ref[...].astype(F32)
        lane = lax.broadcasted_iota(jnp.int32, qb.shape, 1)
        for h in range(HEADS_PER_GROUP):
            lo = BIAS_BASE + BIAS_W * (g * HEADS_PER_GROUP + h)
            qbh = jnp.where((lane >= lo) & (lane < lo + BIAS_W), qb, 0.0).astype(BF16)
            qs_scr[h] = jnp.concatenate([q_ref[:, h * D_HEAD:(h + 1) * D_HEAD], qbh], axis=1)

    args = (qs_scr, k_ref, kb_ref, v_ref, m_scr, acc_scr)
    decode = functools.partial(_dec_update, q16, k_pages, v_pages, None, None, dm, dl, da, None, active)

    @pl.when(ki != qi)
    def _():
        decode()
        _flash_full(*args, tq=tq, nmap=1)

    @pl.when(ki == qi)
    def _():
        decode()
        _flash_diag(*args, tq=tq, nmap=1)
        for h in range(HEADS_PER_GROUP):
            a = acc_scr[h]
            o_ref[:, h * D_HEAD:(h + 1) * D_HEAD] = (a[:, :D_HEAD] / a[:, D_HEAD:]).astype(o_ref.dtype)

    @pl.when(active & (c == st.cps - 1))
    def _():
        o = da[...] / dl[...]
        os_ref[0] = _diff_finish(o[0:N_HEADS], o[N_HEADS:], _diff_lambda(lamp_ref), gn_ref[...])


def _diff_lambda(lamp_ref):
    lp = lamp_ref[...]
    return (jnp.exp(jnp.sum(lp[0:1] * lp[1:2], axis=1, keepdims=True))
            - jnp.exp(jnp.sum(lp[2:3] * lp[3:4], axis=1, keepdims=True)) + LAM_INIT)


def _diff_finish(o1, o2, lam, gn):
    od = o1 - lam * o2
    od = od * lax.rsqrt(jnp.mean(od * od, axis=-1, keepdims=True) + LN_EPS)
    return od * gn * (1.0 - LAM_INIT)


def _diff_kernel(qi_tab, ki_tab, pt_ref, q_ref, k_ref, v_ref, lamp_ref, gn_ref,
                 fq_ref, fk_ref, fv_ref, lfo_ref, ux_ref, *rest, tq, st):
    n = st.pps
    k_pages, v_pages, l_pages = rest[:n], rest[n:2 * n], rest[2 * n:3 * n]
    o_ref, os_ref, qs_scr, m_scr, acc_scr, q16, dm, dl, da, dc = rest[3 * n:]
    g = pl.program_id(0)
    s = pl.program_id(1)
    qi = qi_tab[s]
    ki = ki_tab[s]
    _, _, c, active = _stream_pos(g, s, st)

    @pl.when(active & (c == 0))
    def _():
        _dec_init_fox(fq_ref, fk_ref, fv_ref, lfo_ref, q16, dm, dl, da, dc)

    @pl.when(ki == 0)
    def _():
        _flash_init(m_scr, acc_scr)
        lane = lax.broadcasted_iota(jnp.int32, (tq, D_HEAD), 1)
        for h in range(HEADS_PER_GROUP):
            qh = q_ref[:, h * D_HEAD:(h + 1) * D_HEAD].astype(F32)
            qs_scr[h, 0:tq, :] = jnp.where(lane < D_MAP, qh, 0.0).astype(BF16)
            qs_scr[h, tq:2 * tq, :] = jnp.where(lane >= D_MAP, qh, 0.0).astype(BF16)

    args = (qs_scr, k_ref, None, v_ref, m_scr, acc_scr)
    decode = functools.partial(_dec_update, q16, k_pages, v_pages, l_pages, ux_ref, dm, dl, da, dc, active)

    @pl.when(ki != qi)
    def _():
        decode()
        _flash_full(*args, tq=tq, nmap=2)

    @pl.when(ki == qi)
    def _():
        decode()
        _flash_diag(*args, tq=tq, nmap=2)
        lam = _diff_lambda(lamp_ref)
        gn = gn_ref[...]
        for h in range(HEADS_PER_GROUP):
            a = acc_scr[h]
            o1 = a[0:tq, :D_HEAD] / a[0:tq, D_HEAD:]
            o2 = a[tq:2 * tq, :D_HEAD] / a[tq:2 * tq, D_HEAD:]
            o_ref[:, h * D_HEAD:(h + 1) * D_HEAD] = _diff_finish(o1, o2, lam, gn).astype(o_ref.dtype)

    @pl.when(active & (c == st.cps - 1))
    def _():
        os_ref[0] = (da[...] / dl[...])[0:N_HEADS]


def _causal_steps(n_tiles):
    qi = [q for q in range(n_tiles) for _ in range(q + 1)]
    ki = [k for q in range(n_tiles) for k in range(q + 1)]
    return jnp.asarray(qi, jnp.int32), jnp.asarray(ki, jnp.int32)


def _flash_specs(first_group, tq):
    def spec(grp, by_key_tile):
        return pl.BlockSpec((tq, GROUP_W),
                            lambda g, s, qt, kt, pt: ((kt if by_key_tile else qt)[s], grp * HEAD_GROUPS + g))
    return [spec(first_group, False), spec(first_group + 1, True), spec(first_group + 2, True)]


def _attention_call(kernel_fn, name, st, tq, R, tables, flash_in, flash_specs, seq_in, const_in, const_specs,
                    page_arrays, n_maps, q_width, n_state):
    qi_tab, ki_tab, page_table = tables
    per_seq = pl.BlockSpec((1, N_HEADS, D_HEAD), lambda g, s, qt, kt, pt: (_stream_pos(g, s, st)[1], 0, 0))
    page_specs = []
    for arr in page_arrays:
        page_specs += [_page_spec(arr, r, st) for r in range(st.pps)]
    page_args = [arr for arr in page_arrays for _ in range(st.pps)]
    grid_spec = pltpu.PrefetchScalarGridSpec(
        num_scalar_prefetch=3,
        grid=(HEAD_GROUPS, qi_tab.shape[0]),
        in_specs=flash_specs + [per_seq] * len(seq_in) + const_specs + page_specs,
        out_specs=[pl.BlockSpec((tq, GROUP_W), lambda g, s, qt, kt, pt: (qt[s], g)), per_seq],
        scratch_shapes=(
            [pltpu.VMEM((HEADS_PER_GROUP, n_maps * tq, q_width), BF16),
             pltpu.VMEM((HEADS_PER_GROUP, n_maps * tq, LANE), F32),
             pltpu.VMEM((HEADS_PER_GROUP, n_maps * tq, 2 * D_HEAD), F32),
             pltpu.VMEM((2 * N_HEADS, D_HEAD), BF16)]
            + [pltpu.VMEM((2 * N_HEADS, LANE), F32)] * n_state),
    )
    return pl.pallas_call(
        functools.partial(kernel_fn, tq=tq, st=st),
        grid_spec=grid_spec,
        out_shape=[jax.ShapeDtypeStruct((R, GROUP), BF16),
                   jax.ShapeDtypeStruct((st.ns, N_HEADS, D_HEAD), F32)],
        compiler_params=pltpu.CompilerParams(
            dimension_semantics=("arbitrary", "arbitrary"), vmem_limit_bytes=VMEM_LIMIT),
        name=name,
    )(qi_tab, ki_tab, page_table, *flash_in, *seq_in, *const_in, *page_args)


def _fox_attention(zb, qb, kb, tables, st, tq, dec_seq, lamp, gn, dk_cache, dv_cache):
    const = lambda g, s, qt, kt, pt: (0, 0)
    flash_specs = _flash_specs(0, tq) + [pl.BlockSpec((tq, LANE), lambda g, s, qt, kt, pt: (qt[s], 0)),
                                         pl.BlockSpec((tq, LANE), lambda g, s, qt, kt, pt: (kt[s], 0))]
    return _attention_call(_fox_kernel, "fox_flash", st, tq, zb.shape[0], tables,
                           [zb, zb, zb, qb, kb], flash_specs, dec_seq, [lamp, gn],
                           [pl.BlockSpec((SUBLANE, LANE), const), pl.BlockSpec((1, LANE), const)],
                           [dk_cache, dv_cache], n_maps=1, q_width=2 * D_HEAD, n_state=3)


def _diff_attention(zb, lamp, gn, tables, st, tq, dec_seq, ux, k_cache, v_cache, lf_cache_t):
    const = lambda g, s, qt, kt, pt: (0, 0)
    flash_specs = _flash_specs(N_GROUPS // 2, tq) + [pl.BlockSpec((SUBLANE, LANE), const),
                                                     pl.BlockSpec((1, LANE), const)]
    return _attention_call(_diff_kernel, "diff_flash", st, tq, zb.shape[0], tables,
                           [zb, zb, zb, lamp, gn], flash_specs, dec_seq, [ux],
                           [pl.BlockSpec(ux.shape, const)],
                           [k_cache, v_cache, lf_cache_t], n_maps=2, q_width=D_HEAD, n_state=4)


class _Stream(NamedTuple):
    ns: int
    n_pages: int
    cps: int
    pps: int
    n_pairs: int
    page_rows: int


def _stream_plan(ns, n_pages, n_pairs, page_rows):
    n_steps = HEAD_GROUPS * n_pairs
    fits = [c for c in (8, 4, 2, 1) if ns * c <= n_steps and n_pages % c == 0]
    assert fits, "the attention grid has fewer steps than decode sequences"
    return _Stream(ns=ns, n_pages=n_pages, cps=fits[0], pps=n_pages // fits[0], n_pairs=n_pairs,
                   page_rows=page_rows)


def _stream_pos(g, s, st):
    t = g * st.n_pairs + s
    shift = st.cps.bit_length() - 1
    return t, jnp.minimum(t >> shift, st.ns - 1), t & (st.cps - 1), t < st.ns * st.cps


def _page_spec(cache, r, st):
    def page(g, s, qt, kt, pt):
        _, b, c, _ = _stream_pos(g, s, st)
        return pt[b, st.n_pages - 1 - (c * st.pps + r)]
    if cache.ndim == 3:
        return pl.BlockSpec((1,) + cache.shape[1:], lambda *a: (page(*a), 0, 0))
    return pl.BlockSpec((st.page_rows, D_HEAD), lambda *a: (page(*a), 0))


def _dup(x):
    return jnp.concatenate([x, x], axis=0)


def _dec_init_fox(q_ref, k_ref, v_ref, lfo_ref, q16, m, l, a, c):
    q2 = _dup(q_ref[0])
    q16[...] = q2.astype(BF16)
    m[...] = jnp.broadcast_to(jnp.sum(q2 * _dup(k_ref[0]), axis=1, keepdims=True), m.shape)
    l[...] = jnp.ones(l.shape, F32)
    a[...] = _dup(v_ref[0])
    c[...] = _dup(lfo_ref[0])


def _dec_init_diff(q_ref, k_ref, v_ref, q16, m, l, a):
    lane = lax.broadcasted_iota(jnp.int32, (N_HEADS, D_HEAD), 1)
    qd = q_ref[0]
    q2 = jnp.concatenate([jnp.where(lane < D_MAP, qd, 0.0), jnp.where(lane >= D_MAP, qd, 0.0)], axis=0)
    q16[...] = q2.astype(BF16)
    m[...] = jnp.broadcast_to(jnp.sum(q2 * _dup(k_ref[0]), axis=1, keepdims=True), m.shape)
    l[...] = jnp.ones(l.shape, F32)
    a[...] = _dup(v_ref[0])


def _dec_update(q16, k_pages, v_pages, l_pages, ux_ref, m_ref, l_ref, a_ref, c_ref, active):
    page_rows = k_pages[0].shape[0]
    n_rep = page_rows // LANE
    lane8 = lax.broadcasted_iota(jnp.int32, (2 * N_HEADS, page_rows), 1) & (N_HEADS - 1)
    row8 = lax.broadcasted_iota(jnp.int32, (2 * N_HEADS, page_rows), 0) & (N_HEADS - 1)
    valid = lane8 == row8
    q = q16[...]
    scores = []
    if l_pages is not None:
        ux = ux_ref[...]
        carry = c_ref[...]
        for k_page, l_page in zip(k_pages, l_pages):
            s_all = lax.dot_general(q, k_page[...].astype(BF16), NT_DIMS, preferred_element_type=F32)
            lt = l_page[0]
            pieces = jnp.concatenate(_split3(lt), axis=0).astype(BF16)
            b3 = jnp.dot(pieces, ux, preferred_element_type=F32)
            bias = (_dup(b3[0:N_HEADS] + b3[N_HEADS:2 * N_HEADS] + b3[2 * N_HEADS:3 * N_HEADS])
                    + _rep(carry, n_rep))
            scores.append(jnp.where(valid, s_all + bias, NEG_INF))
            carry = carry + _dup(jnp.sum(lt, axis=1, keepdims=True))
        c_ref[...] = jnp.where(active, carry, c_ref[...])
    else:
        for k_page in k_pages:
            s_all = lax.dot_general(q, k_page[...].astype(BF16), NT_DIMS, preferred_element_type=F32)
            scores.append(jnp.where(valid, s_all, NEG_INF))
    m_prev = m_ref[...]
    m_new = m_prev
    for s in scores:
        m_new = jnp.maximum(m_new, jnp.max(s, axis=1, keepdims=True))
    alpha = jnp.exp(m_prev - m_new)
    m_rep = _rep(m_new, n_rep)
    l_new = alpha * l_ref[...]
    a_new = alpha * a_ref[...]
    for s, v_page in zip(scores, v_pages):
        p = jnp.exp(s - m_rep)
        l_new = l_new + jnp.sum(p, axis=1, keepdims=True)
        a_new = a_new + jnp.dot(p.astype(BF16), v_page[...].astype(BF16), preferred_element_type=F32)
    l_ref[...] = jnp.where(active, l_new, l_ref[...])
    a_ref[...] = jnp.where(active, a_new, a_ref[...])
    m_ref[...] = jnp.where(active, m_new, m_prev)


def _mix_kernel(x_ref, meta_ref, xs_ref, of_ref, od_ref, sf_ref, sd_ref, wt_ref, wb_ref, gi_ref, bi_ref,
                g_ref, b_ref, o_ref, os_ref, *, rows):
    def mixed(x, o_f, o_d):
        y = (ALPHA * _layer_norm(x, gi_ref[...], bi_ref[...])
             + jnp.dot(o_f, wt_ref[...], preferred_element_type=F32)
             + jnp.dot(o_d, wb_ref[...], preferred_element_type=F32))
        return _layer_norm(y, g_ref[...], b_ref[...])

    def body(case):
        o_ref[...] = mixed(_row_space_x(case, x_ref[...], meta_ref, xs_ref, rows), of_ref[...], od_ref[...])
        if case == "last":
            os_ref[...] = mixed(xs_ref[...], sf_ref[...], sd_ref[...])

    _per_tile_case(pl.program_id(0), rows.n_half, body)


def _mix_out(x_prompt, meta, x_sample, o_f, o_d, os_f, os_d, wo_top, wo_bot, gi, bi, g, b, rows):
    D = x_prompt.shape[1]
    tm = rows.tmh
    row = lambda i: (i, 0)
    const = lambda i: (0, 0)
    return pl.pallas_call(
        functools.partial(_mix_kernel, rows=rows),
        grid=(rows.n_half,),
        in_specs=[_x_block_spec(rows, D), pl.BlockSpec(meta.shape, const), pl.BlockSpec(x_sample.shape, const),
                  pl.BlockSpec((tm, GROUP), row), pl.BlockSpec((tm, GROUP), row),
                  pl.BlockSpec((rows.ns, GROUP), const), pl.BlockSpec((rows.ns, GROUP), const),
                  pl.BlockSpec((GROUP, D), const), pl.BlockSpec((GROUP, D), const),
                  pl.BlockSpec((1, D), const), pl.BlockSpec((1, D), const),
                  pl.BlockSpec((1, D), const), pl.BlockSpec((1, D), const)],
        out_specs=[pl.BlockSpec((tm, D), row), pl.BlockSpec((rows.ns, D), const)],
        out_shape=[jax.ShapeDtypeStruct((rows.r, D), F32),
                   jax.ShapeDtypeStruct((rows.ns, D), F32)],
        compiler_params=pltpu.CompilerParams(dimension_semantics=("arbitrary",), vmem_limit_bytes=VMEM_LIMIT),
        name="mix_out_ln1",
    )(x_prompt, meta, x_sample, o_f, o_d, os_f, os_d, wo_top, wo_bot, gi, bi, g, b)


def _ffn_kernel(h_ref, wu_ref, wd_ref, g_ref, b_ref, o_ref, hb_scr, acc_scr):
    j = pl.program_id(1)

    @pl.when(j == 0)
    def _():
        hb_scr[...] = h_ref[...].astype(BF16)
        acc_scr[...] = jnp.zeros_like(acc_scr)

    u = jnp.maximum(jnp.dot(hb_scr[...], wu_ref[...], preferred_element_type=F32), 0.0)
    acc_scr[...] += jnp.dot((u * u).astype(BF16), wd_ref[...], preferred_element_type=F32)

    @pl.when(j == pl.num_programs(1) - 1)
    def _():
        o_ref[...] = _layer_norm(ALPHA * h_ref[...] + acc_scr[...], g_ref[...], b_ref[...])


def _ffn(h1, w_up, w_down, g, b, n_rows, row0, tm, tf):
    D = h1.shape[1]
    d_ff = w_up.shape[1]
    return pl.pallas_call(
        _ffn_kernel,
        grid=(n_rows // tm, d_ff // tf),
        in_specs=[pl.BlockSpec((pl.Element(tm), pl.Element(D)),
                               lambda i, j: (pl.multiple_of(row0 + i * tm, SUBLANE), 0)),
                  pl.BlockSpec((D, tf), lambda i, j: (0, j)),
                  pl.BlockSpec((tf, D), lambda i, j: (j, 0)),
                  pl.BlockSpec((1, D), lambda i, j: (0, 0)),
                  pl.BlockSpec((1, D), lambda i, j: (0, 0))],
        out_specs=pl.BlockSpec((tm, D), lambda i, j: (i, 0)),
        out_shape=jax.ShapeDtypeStruct((n_rows, D), F32),
        scratch_shapes=[pltpu.VMEM((tm, D), BF16), pltpu.VMEM((tm, D), F32)],
        compiler_params=pltpu.CompilerParams(
            dimension_semantics=("arbitrary", "arbitrary"), vmem_limit_bytes=VMEM_LIMIT),
        name="ffn_ln2",
    )(h1, w_up, w_down, g, b)


def _rope_tables(pos):
    half = ROT_DIM // 2
    inv_freq = jnp.power(jnp.float32(ROPE_THETA), -jnp.arange(0, ROT_DIM, 2, dtype=F32) / ROT_DIM)
    ang = pos.astype(F32)[:, None] * inv_freq[None, :]
    cos, sin = jnp.cos(ang), jnp.sin(ang)
    lane = jnp.arange(D_HEAD) % D_MAP
    f = lane % half
    first, second = lane < half, (lane >= half) & (lane < ROT_DIM)
    cos_t = jnp.where((first | second)[None, :], cos[:, f], 1.0)
    sa_t = jnp.where(first[None, :], -sin[:, f], 0.0)
    sb_t = jnp.where(second[None, :], sin[:, f], 0.0)
    return cos_t, sa_t, sb_t


def kernel(x_prompt, x_sample, cache_fox_k, cache_fox_v, cache_fox_logf, cache_diff_k, cache_diff_v, page_table, meta_tokens, ln_in_g, ln_in_b, w_in, b_forget, lambda_q1, lambda_k1, lambda_q2, lambda_k2, diff_norm_g, w_o, ln1_g, ln1_b, w_up, w_down, ln2_g, ln2_b):
    batch, seq, D = x_prompt.shape
    NS, dec_seq, _ = x_sample.shape
    depth, n_pool, page_size = cache_fox_k.shape[:3]
    n_pages = page_table.shape[1]
    assert batch == 1 and dec_seq == 1 and depth == DEPTH
    assert cache_fox_k.shape[3:] == (N_HEADS, D_HEAD) and meta_tokens.shape[0] == N_META
    past_len = n_pages * page_size

    tm = ROW_TILE
    Lr = N_META + seq
    S0 = _round_up(Lr, 32)
    R = _round_up(S0 + NS, tm)
    rows = _Rows(seq=seq, ns=NS, lr=Lr, s0=S0, r=R, tmh=tm // 2)
    assert rows.n_half >= 2 and R - rows.tmh <= Lr and S0 >= R - rows.tmh and seq >= rows.tmh
    xp, xs, meta = x_prompt[0], x_sample[:, 0], meta_tokens.astype(F32)
    pos = jnp.concatenate([jnp.arange(Lr), jnp.zeros((S0 - Lr,), jnp.int32),
                           jnp.full((NS,), past_len), jnp.zeros((R - S0 - NS,), jnp.int32)])
    cos_t, sa_t, sb_t = _rope_tables(pos)

    w_fox = w_in[0][:, :N_GROUPS // 2 * GROUP].astype(BF16)
    w_diff = w_in[0][:, N_GROUPS // 2 * GROUP + N_HEADS:].astype(BF16)
    wf = w_in[0][:, N_GROUPS // 2 * GROUP:N_GROUPS // 2 * GROUP + N_HEADS]
    lane = np.arange(LANE)
    bias_lane = (lane >= BIAS_BASE) & (lane < BIAS_BASE + BIAS_W * N_HEADS)
    head_of_lane = np.where(bias_lane, (lane - BIAS_BASE) // BIAS_W, lane % N_HEADS)
    used = bias_lane | (lane < N_HEADS)
    w_f = jnp.where(used[None, :], wf[:, head_of_lane], 0.0).astype(BF16)
    b_f = jnp.where(used, b_forget[0].astype(F32)[head_of_lane], 0.0)[None, :]
    lane_j = jnp.asarray(np.where(bias_lane, (lane - BIAS_BASE) % BIAS_W, -1)[None, :], jnp.int32)
    row2 = lambda v: v.astype(F32).reshape(1, -1)
    gi, bi = row2(ln_in_g), row2(ln_in_b)

    hb, qb, kb, lf = _ln_gate(xp, meta, xs, gi, bi, w_f, b_f, lane_j, rows)
    zb, p_fk, p_fv, p_dk, p_dv, s_fk, s_fv, s_dk, s_dv = _project(hb, w_fox, w_diff, cos_t, sa_t, sb_t,
                                                                  tm, Lr, S0, NS)

    lamp = jnp.zeros((SUBLANE, LANE), F32).at[0:4, 0:D_MAP].set(
        jnp.stack([lambda_q1[0], lambda_k1[0], lambda_q2[0], lambda_k2[0]]).astype(F32))
    gn = row2(diff_norm_g[0])

    grp = lambda g: zb[S0:S0 + NS, g * GROUP:(g + 1) * GROUP].reshape(NS, N_HEADS, D_HEAD).astype(F32)
    lf_own = jnp.broadcast_to(lf[S0:S0 + NS, :, None], (NS, N_HEADS, LANE))
    page_rows = page_size * N_HEADS
    flat = lambda c: c[0].reshape(n_pool * page_rows, D_HEAD)
    lf_cache_t = jnp.swapaxes(cache_fox_logf[0], 1, 2)
    s_idx = jnp.arange(page_size)
    ux = (s_idx[:, None] > jnp.repeat(s_idx, N_HEADS)[None, :]).astype(BF16)
    qi_tab, ki_tab = _causal_steps(R // tm)
    st = _stream_plan(NS, n_pages, qi_tab.shape[0], page_rows)
    tables = (qi_tab, ki_tab, page_table)
    o_f, os_d = _fox_attention(zb, qb, kb, tables, st, tm, (grp(3), grp(4), grp(5)), lamp, gn,
                               flat(cache_diff_k), flat(cache_diff_v))
    o_d, os_f = _diff_attention(zb, lamp, gn, tables, st, tm, (grp(0), grp(1), grp(2), lf_own), ux,
                                flat(cache_fox_k), flat(cache_fox_v), lf_cache_t)

    wo = w_o[0].astype(BF16)
    h1, h1_s = _mix_out(xp, meta, xs, o_f, o_d, os_f.reshape(NS, GROUP).astype(BF16),
                        os_d.reshape(NS, GROUP).astype(BF16), wo[:GROUP], wo[GROUP:], gi, bi,
                        row2(ln1_g[0]), row2(ln1_b[0]), rows)
    ffn = functools.partial(_ffn, w_up=w_up[0].astype(BF16), w_down=w_down[0].astype(BF16),
                            g=row2(ln2_g[0]), b=row2(ln2_b[0]), tf=1024)
    tm_ffn = next(t for t in (512, 256, 128) if seq % t == 0)
    y_prompt = ffn(h1, n_rows=seq, row0=N_META, tm=tm_ffn)
    y_sample = ffn(h1_s, n_rows=NS, row0=0, tm=NS)

    prompt_cache = lambda a: a.reshape(1, 1, Lr, N_HEADS, D_HEAD)
    sample_cache = lambda a: a.reshape(1, NS, 1, N_HEADS, D_HEAD)
    return (y_prompt[None], y_sample[:, None],
            prompt_cache(p_fk), prompt_cache(p_fv), lf[:Lr][None, None], prompt_cache(p_dk), prompt_cache(p_dv),
            sample_cache(s_fk), sample_cache(s_fv), lf[S0:S0 + NS][None, :, None],
            sample_cache(s_dk), sample_cache(s_dv))
```

```python
import functools
import math
from typing import NamedTuple

import jax
import jax.numpy as jnp
import numpy as np
from jax import lax
from jax.experimental import pallas as pl
from jax.experimental.pallas import tpu as pltpu

F32 = jnp.float32
BF16 = jnp.bfloat16

N_META = 16
N_HEADS = 8
D_HEAD = 128
D_MAP = 64
ROT_DIM = D_MAP // 4
ROPE_THETA = 500000.0
DEPTH = 1
ALPHA = (2 * DEPTH) ** 0.25
LN_EPS = 1e-5
NEG_INF = -1e30
LAM_INIT = 0.8 - 0.6 * math.exp(-0.3 * 0)
FOX_SCALE = D_HEAD ** -0.5
DIFF_SCALE = D_MAP ** -0.5
GROUP = N_HEADS * D_HEAD
N_GROUPS = 6
BIAS_BASE = N_HEADS
BIAS_W = 6

LANE = 128
SUBLANE = 8
MXU_DEPTH = 256
VMEM_LIMIT = 56 * 1024 * 1024
HEAD_GROUPS = 2
HEADS_PER_GROUP = N_HEADS // HEAD_GROUPS
GROUP_W = HEADS_PER_GROUP * D_HEAD
DIAG_BLOCKS = 3
ROW_TILE = DIAG_BLOCKS * MXU_DEPTH

NT_DIMS = (((1,), (1,)), ((), ()))


def _round_up(x, m):
    return (x + m - 1) // m * m


def _layer_norm(x, g, b):
    mu = jnp.mean(x, -1, keepdims=True)
    xc = x - mu
    var = jnp.mean(xc * xc, -1, keepdims=True)
    return xc * lax.rsqrt(var + LN_EPS) * g + b


def _split3(x):
    hi = x.astype(BF16).astype(F32)
    r1 = x - hi
    mid = r1.astype(BF16).astype(F32)
    lo = (r1 - mid).astype(BF16).astype(F32)
    return hi, mid, lo


def _rep(x, n):
    return jnp.concatenate([x] * n, axis=1)


class _Rows(NamedTuple):
    seq: int
    ns: int
    lr: int
    s0: int
    r: int
    tmh: int

    @property
    def n_half(self):
        return self.r // self.tmh


def _x_block_spec(rows, d):
    def index(i, *_):
        return pl.multiple_of(jnp.clip(i * rows.tmh - N_META, 0, rows.seq - rows.tmh), SUBLANE), 0
    return pl.BlockSpec((pl.Element(rows.tmh), pl.Element(d)), index)


def _row_space_x(case, blk, meta_ref, xs_ref, rows):
    if case == "first":
        return jnp.concatenate([meta_ref[...], blk[0:rows.tmh - N_META]], axis=0)
    if case == "mid":
        return blk
    n_prompt = rows.lr - (rows.r - rows.tmh)
    parts = [blk[rows.tmh - n_prompt:rows.tmh]]
    for n, val in ((rows.s0 - rows.lr, None), (rows.ns, xs_ref), (rows.r - rows.s0 - rows.ns, None)):
        if n:
            parts.append(jnp.zeros((n, blk.shape[1]), F32) if val is None else val[...])
    return jnp.concatenate(parts, axis=0)


def _per_tile_case(i, n, body):
    pl.when(i == 0)(functools.partial(body, "first"))
    pl.when((i > 0) & (i < n - 1))(functools.partial(body, "mid"))
    pl.when(i == n - 1)(functools.partial(body, "last"))


def _gate_kernel(x_ref, meta_ref, xs_ref, g_ref, b_ref, wf_ref, bf_ref, lanej_ref,
                 hb_ref, qb_ref, kb_ref, lf_ref, carry_scr, *, rows):
    i = pl.program_id(0)
    tm = rows.tmh

    @pl.when(i == 0)
    def _():
        carry_scr[...] = jnp.zeros_like(carry_scr)

    def body(case):
        x = _row_space_x(case, x_ref[...], meta_ref, xs_ref, rows)
        hb = _layer_norm(x, g_ref[...], b_ref[...]).astype(BF16)
        hb_ref[...] = hb
        ff = jnp.dot(hb, wf_ref[...], preferred_element_type=F32) + bf_ref[...]
        lf = jnp.minimum(ff, 0.0) - jnp.log1p(jnp.exp(-jnp.abs(ff)))
        lf_ref[...] = lf[:, 0:N_HEADS]
        row = lax.broadcasted_iota(jnp.int32, (tm, tm), 0)
        col = lax.broadcasted_iota(jnp.int32, (tm, tm), 1)
        tri = (col <= row).astype(F32).astype(BF16)
        pieces = jnp.concatenate(_split3(lf), axis=1).astype(BF16)
        c3 = jnp.dot(tri, pieces, preferred_element_type=F32)
        cs = c3[:, 0:LANE] + c3[:, LANE:2 * LANE] + c3[:, 2 * LANE:3 * LANE] + carry_scr[0:1, :]
        carry_scr[...] = jnp.broadcast_to(cs[tm - 1:tm, :], carry_scr.shape)
        hi, mid, lo = _split3(cs)
        j = jnp.broadcast_to(lanej_ref[...], (tm, LANE))
        one = jnp.ones((tm, LANE), F32)
        zero = jnp.zeros((tm, LANE), F32)
        qb_ref[...] = jnp.where(j == 0, hi, jnp.where(j == 1, mid, jnp.where(
            j == 2, lo, jnp.where(j >= 3, one, zero)))).astype(BF16)
        kb_ref[...] = jnp.where(j == 3, -hi, jnp.where(j == 4, -mid, jnp.where(
            j == 5, -lo, jnp.where(j >= 0, one, zero)))).astype(BF16)

    _per_tile_case(i, rows.n_half, body)


def _ln_gate(x_prompt, meta, x_sample, ln_g, ln_b, w_f, b_f, lane_j, rows):
    D = x_prompt.shape[1]
    tm = rows.tmh
    row = lambda i: (i, 0)
    const = lambda i: (0, 0)
    return pl.pallas_call(
        functools.partial(_gate_kernel, rows=rows),
        grid=(rows.n_half,),
        in_specs=[_x_block_spec(rows, D), pl.BlockSpec(meta.shape, const), pl.BlockSpec(x_sample.shape, const),
                  pl.BlockSpec((1, D), const), pl.BlockSpec((1, D), const),
                  pl.BlockSpec((D, LANE), const), pl.BlockSpec((1, LANE), const), pl.BlockSpec((1, LANE), const)],
        out_specs=[pl.BlockSpec((tm, D), row), pl.BlockSpec((tm, LANE), row),
                   pl.BlockSpec((tm, LANE), row), pl.BlockSpec((tm, N_HEADS), row)],
        out_shape=[jax.ShapeDtypeStruct((rows.r, D), BF16),
                   jax.ShapeDtypeStruct((rows.r, LANE), BF16),
                   jax.ShapeDtypeStruct((rows.r, LANE), BF16),
                   jax.ShapeDtypeStruct((rows.r, N_HEADS), F32)],
        scratch_shapes=[pltpu.VMEM((SUBLANE, LANE), F32)],
        compiler_params=pltpu.CompilerParams(dimension_semantics=("arbitrary",), vmem_limit_bytes=VMEM_LIMIT),
        name="ln_gate",
    )(x_prompt, meta, x_sample, ln_g, ln_b, w_f, b_f, lane_j)


def _rope(zh, cos, sa, sb):
    return zh * cos + pltpu.roll(zh, D_HEAD - ROT_DIM // 2, 1) * sa + pltpu.roll(zh, ROT_DIM // 2, 1) * sb


def _proj_kernel(hb_ref, wfox_ref, wdiff_ref, cos_ref, sa_ref, sb_ref, zb_ref, pk_ref, pv_ref, pdk_ref, pdv_ref,
                 sk_ref, sv_ref, sdk_ref, sdv_ref, *, tm, s_tile, s_off, ns):
    j = pl.program_id(0)
    i = pl.program_id(1)
    heads = [slice(h * D_HEAD, (h + 1) * D_HEAD) for h in range(N_HEADS)]

    def matmul(w_ref):
        return jnp.dot(hb_ref[...], w_ref[...], preferred_element_type=F32)

    def cache_store(per_head, p_ref, s_ref):
        for h in range(N_HEADS):
            p_ref[pl.ds(h, tm, stride=N_HEADS), :] = per_head[h]

        @pl.when(i == s_tile)
        def _():
            for h in range(N_HEADS):
                s_ref[pl.ds(h, ns, stride=N_HEADS), :] = per_head[h][s_off:s_off + ns]

    @pl.when(j == 0)
    def _():
        zb_ref[...] = (matmul(wfox_ref) * FOX_SCALE).astype(BF16)

    def plain(w_ref, p_ref, s_ref):
        z = matmul(w_ref)
        zb_ref[...] = z.astype(BF16)
        cache_store([z[:, hs] for hs in heads], p_ref, s_ref)

    pl.when(j == 1)(functools.partial(plain, wfox_ref, pk_ref, sk_ref))
    pl.when(j == 2)(functools.partial(plain, wfox_ref, pv_ref, sv_ref))
    pl.when(j == 5)(functools.partial(plain, wdiff_ref, pdv_ref, sdv_ref))

    @pl.when(j == 3)
    def _():
        z = matmul(wdiff_ref)
        cos, sa, sb = cos_ref[...], sa_ref[...], sb_ref[...]
        for hs in heads:
            zb_ref[:, hs] = (_rope(z[:, hs], cos, sa, sb) * DIFF_SCALE).astype(BF16)

    @pl.when(j == 4)
    def _():
        z = matmul(wdiff_ref)
        cos, sa, sb = cos_ref[...], sa_ref[...], sb_ref[...]
        ys = [_rope(z[:, hs], cos, sa, sb) for hs in heads]
        for hs, y in zip(heads, ys):
            zb_ref[:, hs] = y.astype(BF16)
        cache_store(ys, pdk_ref, sdk_ref)


def _project(hb, w_fox, w_diff, cos_t, sa_t, sb_t, tm, n_prompt_rows, s0, ns):
    R, D = hb.shape
    n = R // tm
    half = N_GROUPS // 2
    row = lambda j, i: (i, 0)

    def cache_spec(group):
        return pl.BlockSpec((tm * N_HEADS, D_HEAD),
                            lambda j, i: (jnp.where(j < group, 0, jnp.where(j == group, i, n - 1)), 0))

    sample_spec = pl.BlockSpec((ns * N_HEADS, D_HEAD), lambda j, i: (0, 0))
    cache_shape = jax.ShapeDtypeStruct((n_prompt_rows * N_HEADS, D_HEAD), F32)
    sample_shape = jax.ShapeDtypeStruct((ns * N_HEADS, D_HEAD), F32)
    return pl.pallas_call(
        functools.partial(_proj_kernel, tm=tm, s_tile=s0 // tm, s_off=s0 % tm, ns=ns),
        grid=(N_GROUPS, n),
        in_specs=[pl.BlockSpec((tm, D), row),
                  pl.BlockSpec((D, GROUP), lambda j, i: (0, jnp.minimum(j, half - 1))),
                  pl.BlockSpec((D, GROUP), lambda j, i: (0, jnp.maximum(j - half, 0))),
                  pl.BlockSpec((tm, LANE), row), pl.BlockSpec((tm, LANE), row), pl.BlockSpec((tm, LANE), row)],
        out_specs=[pl.BlockSpec((tm, GROUP), lambda j, i: (i, j)),
                   cache_spec(1), cache_spec(2), cache_spec(4), cache_spec(5),
                   sample_spec, sample_spec, sample_spec, sample_spec],
        out_shape=[jax.ShapeDtypeStruct((R, N_GROUPS * GROUP), BF16),
                   cache_shape, cache_shape, cache_shape, cache_shape,
                   sample_shape, sample_shape, sample_shape, sample_shape],
        compiler_params=pltpu.CompilerParams(
            dimension_semantics=("arbitrary", "arbitrary"), vmem_limit_bytes=VMEM_LIMIT),
        name="in_proj",
    )(hb, w_fox, w_diff, cos_t, sa_t, sb_t)


def _flash_block(qs_scr, k_ref, kb_ref, v_ref, m_scr, acc_scr, h, r0, nr, nc, keep):
    hs = slice(h * D_HEAD, (h + 1) * D_HEAD)
    kh = k_ref[0:nc, hs]
    if kb_ref is not None:
        kh = jnp.concatenate([kh, kb_ref[0:nc, :]], axis=1)
    s = lax.dot_general(qs_scr[h, r0:r0 + nr, :], kh, NT_DIMS, preferred_element_type=F32)
    if keep is not None:
        s = jnp.where(keep, s, NEG_INF)
    m_prev = m_scr[h, r0:r0 + nr, :]
    m_new = jnp.maximum(m_prev, jnp.max(s, axis=1, keepdims=True))
    p = jnp.exp(s - _rep(m_new, nc // LANE))
    alpha = jnp.exp(m_prev - m_new)
    vh = jnp.concatenate([v_ref[0:nc, hs], jnp.ones((nc, D_HEAD), BF16)], axis=1)
    pv = jnp.dot(p.astype(BF16), vh, preferred_element_type=F32)
    acc_scr[h, r0:r0 + nr, :] = acc_scr[h, r0:r0 + nr, :] * _rep(alpha, 2) + pv
    m_scr[h, r0:r0 + nr, :] = m_new


def _flash_full(qs_scr, k_ref, kb_ref, v_ref, m_scr, acc_scr, *, tq, nmap):
    for h in range(HEADS_PER_GROUP):
        _flash_block(qs_scr, k_ref, kb_ref, v_ref, m_scr, acc_scr, h, 0, nmap * tq, tq, None)


def _flash_diag(qs_scr, k_ref, kb_ref, v_ref, m_scr, acc_scr, *, tq, nmap):
    sub = tq // DIAG_BLOCKS
    for h in range(HEADS_PER_GROUP):
        for rb in range(DIAG_BLOCKS):
            nc = (rb + 1) * sub
            row = lax.broadcasted_iota(jnp.int32, (sub, nc), 0) + rb * sub
            keep = lax.broadcasted_iota(jnp.int32, (sub, nc), 1) <= row
            for mp in range(nmap):
                _flash_block(qs_scr, k_ref, kb_ref, v_ref, m_scr, acc_scr, h, mp * tq + rb * sub, sub, nc, keep)


def _flash_init(m_scr, acc_scr):
    m_scr[...] = jnp.full(m_scr.shape, NEG_INF, F32)
    acc_scr[...] = jnp.zeros(acc_scr.shape, F32)


def _fox_kernel(qi_tab, ki_tab, pt_ref, q_ref, k_ref, v_ref, qb_ref, kb_ref,
                dq_ref, dk_ref, dv_ref, lamp_ref, gn_ref, *rest, tq, st):
    n = st.pps
    k_pages, v_pages = rest[:n], rest[n:2 * n]
    o_ref, os_ref, qs_scr, m_scr, acc_scr, q16, dm, dl, da = rest[2 * n:]
    g = pl.program_id(0)
    s = pl.program_id(1)
    qi = qi_tab[s]
    ki = ki_tab[s]
    _, _, c, active = _stream_pos(g, s, st)

    @pl.when(active & (c == 0))
    def _():
        _dec_init_diff(dq_ref, dk_ref, dv_ref, q16, dm, dl, da)

    @pl.when(ki == 0)
    def _():
        _flash_init(m_scr, acc_scr)
        qb = qb_ref[...].astype(F32)
        lane = lax.broadcasted_iota(jnp.int32, qb.shape, 1)
        for h in range(HEADS_PER_GROUP):
            lo = BIAS_BASE + BIAS_W * (g * HEADS_PER_GROUP + h)
            qbh = jnp.where((lane >= lo) & (lane < lo + BIAS_W), qb, 0.0).astype(BF16)
            qs_scr[h] = jnp.concatenate([q_ref[:, h * D_HEAD:(h + 1) * D_HEAD], qbh], axis=1)

    args = (qs_scr, k_ref, kb_ref, v_ref, m_scr, acc_scr)
    decode = functools.partial(_dec_update_diff, q16, k_pages, v_pages, dm, dl, da, active)

    @pl.when(ki != qi)
    def _():
        decode()
        _flash_full(*args, tq=tq, nmap=1)

    @pl.when(ki == qi)
    def _():
        decode()
        _flash_diag(*args, tq=tq, nmap=1)
        for h in range(HEADS_PER_GROUP):
            a = acc_scr[h]
            o_ref[:, h * D_HEAD:(h + 1) * D_HEAD] = (a[:, :D_HEAD] / a[:, D_HEAD:]).astype(o_ref.dtype)

    @pl.when(active & (c == st.cps - 1))
    def _():
        o = da[...] / dl[...]
        os_ref[0] = _diff_finish(o[0:N_HEADS], o[N_HEADS:], _diff_lambda(lamp_ref), gn_ref[...])


def _diff_lambda(lamp_ref):
    lp = lamp_ref[...]
    return (jnp.exp(jnp.sum(lp[0:1] * lp[1:2], axis=1, keepdims=True))
            - jnp.exp(jnp.sum(lp[2:3] * lp[3:4], axis=1, keepdims=True)) + LAM_INIT)


def _diff_finish(o1, o2, lam, gn):
    od = o1 - lam * o2
    od = od * lax.rsqrt(jnp.mean(od * od, axis=-1, keepdims=True) + LN_EPS)
    return od * gn * (1.0 - LAM_INIT)


def _diff_kernel(qi_tab, ki_tab, pt_ref, q_ref, k_ref, v_ref, lamp_ref, gn_ref,
                 fq_ref, fk_ref, fv_ref, lfo_ref, ux_ref, *rest, tq, st):
    n = st.pps
    k_pages, v_pages, l_pages = rest[:n], rest[n:2 * n], rest[2 * n:3 * n]
    o_ref, os_ref, qs_scr, m_scr, acc_scr, q16, dm, dl, da, dc = rest[3 * n:]
    g = pl.program_id(0)
    s = pl.program_id(1)
    qi = qi_tab[s]
    ki = ki_tab[s]
    _, _, c, active = _stream_pos(g, s, st)

    @pl.when(active & (c == 0))
    def _():
        _dec_init_fox(fq_ref, fk_ref, fv_ref, lfo_ref, q16, dm, dl, da, dc)

    @pl.when(ki == 0)
    def _():
        _flash_init(m_scr, acc_scr)
        lane = lax.broadcasted_iota(jnp.int32, (tq, D_HEAD), 1)
        for h in range(HEADS_PER_GROUP):
            qh = q_ref[:, h * D_HEAD:(h + 1) * D_HEAD].astype(F32)
            qs_scr[h, 0:tq, :] = jnp.where(lane < D_MAP, qh, 0.0).astype(BF16)
            qs_scr[h, tq:2 * tq, :] = jnp.where(lane >= D_MAP, qh, 0.0).astype(BF16)

    args = (qs_scr, k_ref, None, v_ref, m_scr, acc_scr)
    decode = functools.partial(_dec_update_fox, q16, k_pages, v_pages, l_pages, ux_ref, dm, dl, da, dc, active)

    @pl.when(ki != qi)
    def _():
        decode()
        _flash_full(*args, tq=tq, nmap=2)

    @pl.when(ki == qi)
    def _():
        decode()
        _flash_diag(*args, tq=tq, nmap=2)
        lam = _diff_lambda(lamp_ref)
        gn = gn_ref[...]
        for h in range(HEADS_PER_GROUP):
            a = acc_scr[h]
            o1 = a[0:tq, :D_HEAD] / a[0:tq, D_HEAD:]
            o2 = a[tq:2 * tq, :D_HEAD] / a[tq:2 * tq, D_HEAD:]
            o_ref[:, h * D_HEAD:(h + 1) * D_HEAD] = _diff_finish(o1, o2, lam, gn).astype(o_ref.dtype)

    @pl.when(active & (c == st.cps - 1))
    def _():
        os_ref[0] = (da[...] / dl[...])[0:N_HEADS]


def _causal_steps(n_tiles):
    qi = [q for q in range(n_tiles) for _ in range(q + 1)]
    ki = [k for q in range(n_tiles) for k in range(q + 1)]
    return jnp.asarray(qi, jnp.int32), jnp.asarray(ki, jnp.int32)


def _flash_specs(first_group, tq):
    def spec(grp, by_key_tile):
        return pl.BlockSpec((tq, GROUP_W),
                            lambda g, s, qt, kt, pt: ((kt if by_key_tile else qt)[s], grp * HEAD_GROUPS + g))
    return [spec(first_group, False), spec(first_group + 1, True), spec(first_group + 2, True)]


def _attention_call(kernel_fn, name, st, tq, R, tables, flash_in, flash_specs, seq_in, const_in, const_specs,
                    page_arrays, n_maps, q_width, n_state):
    qi_tab, ki_tab, page_table = tables
    per_seq = pl.BlockSpec((1, N_HEADS, D_HEAD), lambda g, s, qt, kt, pt: (_stream_pos(g, s, st)[1], 0, 0))
    page_specs = []
    for arr in page_arrays:
        page_specs += [_page_spec(arr, r, st) for r in range(st.pps)]
    page_args = [arr for arr in page_arrays for _ in range(st.pps)]
    grid_spec = pltpu.PrefetchScalarGridSpec(
        num_scalar_prefetch=3,
        grid=(HEAD_GROUPS, qi_tab.shape[0]),
        in_specs=flash_specs + [per_seq] * len(seq_in) + const_specs + page_specs,
        out_specs=[pl.BlockSpec((tq, GROUP_W), lambda g, s, qt, kt, pt: (qt[s], g)), per_seq],
        scratch_shapes=(
            [pltpu.VMEM((HEADS_PER_GROUP, n_maps * tq, q_width), BF16),
             pltpu.VMEM((HEADS_PER_GROUP, n_maps * tq, LANE), F32),
             pltpu.VMEM((HEADS_PER_GROUP, n_maps * tq, 2 * D_HEAD), F32),
             pltpu.VMEM((2 * N_HEADS, D_HEAD), BF16)]
            + [pltpu.VMEM((2 * N_HEADS, LANE), F32)] * n_state),
    )
    return pl.pallas_call(
        functools.partial(kernel_fn, tq=tq, st=st),
        grid_spec=grid_spec,
        out_shape=[jax.ShapeDtypeStruct((R, GROUP), BF16),
                   jax.ShapeDtypeStruct((st.ns, N_HEADS, D_HEAD), F32)],
        compiler_params=pltpu.CompilerParams(
            dimension_semantics=("arbitrary", "arbitrary"), vmem_limit_bytes=VMEM_LIMIT),
        name=name,
    )(qi_tab, ki_tab, page_table, *flash_in, *seq_in, *const_in, *page_args)


def _fox_attention(zb, qb, kb, tables, st, tq, dec_seq, lamp, gn, dk_cache, dv_cache):
    const = lambda g, s, qt, kt, pt: (0, 0)
    flash_specs = _flash_specs(0, tq) + [pl.BlockSpec((tq, LANE), lambda g, s, qt, kt, pt: (qt[s], 0)),
                                         pl.BlockSpec((tq, LANE), lambda g, s, qt, kt, pt: (kt[s], 0))]
    return _attention_call(_fox_kernel, "fox_flash", st, tq, zb.shape[0], tables,
                           [zb, zb, zb, qb, kb], flash_specs, dec_seq, [lamp, gn],
                           [pl.BlockSpec((SUBLANE, LANE), const), pl.BlockSpec((1, LANE), const)],
                           [dk_cache, dv_cache], n_maps=1, q_width=2 * D_HEAD, n_state=3)


def _diff_attention(zb, lamp, gn, tables, st, tq, dec_seq, ux, k_cache, v_cache, lf_cache_t):
    const = lambda g, s, qt, kt, pt: (0, 0)
    flash_specs = _flash_specs(N_GROUPS // 2, tq) + [pl.BlockSpec((SUBLANE, LANE), const),
                                                     pl.BlockSpec((1, LANE), const)]
    return _attention_call(_diff_kernel, "diff_flash", st, tq, zb.shape[0], tables,
                           [zb, zb, zb, lamp, gn], flash_specs, dec_seq, [ux], [pl.BlockSpec(ux.shape, const)],
                           [k_cache, v_cache, lf_cache_t], n_maps=2, q_width=D_HEAD, n_state=4)


class _Stream(NamedTuple):
    ns: int
    n_pages: int
    cps: int
    pps: int
    n_pairs: int
    page_rows: int


def _stream_plan(ns, n_pages, n_pairs, page_rows):
    n_steps = HEAD_GROUPS * n_pairs
    fits = [c for c in (8, 4, 2, 1) if ns * c <= n_steps and n_pages % c == 0]
    assert fits, "the attention grid has fewer steps than decode sequences"
    return _Stream(ns=ns, n_pages=n_pages, cps=fits[0], pps=n_pages // fits[0], n_pairs=n_pairs,
                   page_rows=page_rows)


def _step_pages(page_table, st):
    t = np.arange(HEAD_GROUPS * st.n_pairs)
    b = np.minimum(t // st.cps, st.ns - 1)
    cols = st.n_pages - 1 - ((t % st.cps)[:, None] * st.pps + np.arange(st.pps)[None, :])
    return page_table[b[:, None], cols].reshape(-1)


def _stream_pos(g, s, st):
    t = g * st.n_pairs + s
    shift = st.cps.bit_length() - 1
    return t, jnp.minimum(t >> shift, st.ns - 1), t & (st.cps - 1), t < st.ns * st.cps


def _page_spec(cache, r, st):
    def page(g, s, qt, kt, pt):
        return pt[(g * st.n_pairs + s) * st.pps + r]
    if cache.ndim == 3:
        return pl.BlockSpec((1,) + cache.shape[1:], lambda *a: (page(*a), 0, 0))
    return pl.BlockSpec((st.page_rows, D_HEAD), lambda *a: (page(*a), 0))


def _dup(x):
    return jnp.concatenate([x, x], axis=0)


def _dec_init_fox(q_ref, k_ref, v_ref, lfo_ref, q16, m, l, a, c):
    q2 = _dup(q_ref[0])
    q16[...] = q2.astype(BF16)
    m[...] = jnp.broadcast_to(jnp.sum(q2 * _dup(k_ref[0]), axis=1, keepdims=True), m.shape)
    l[...] = jnp.ones(l.shape, F32)
    a[...] = _dup(v_ref[0])
    c[...] = _dup(lfo_ref[0])


def _dec_init_diff(q_ref, k_ref, v_ref, q16, m, l, a):
    lane = lax.broadcasted_iota(jnp.int32, (N_HEADS, D_HEAD), 1)
    qd = q_ref[0]
    q2 = jnp.concatenate([jnp.where(lane < D_MAP, qd, 0.0), jnp.where(lane >= D_MAP, qd, 0.0)], axis=0)
    q16[...] = q2.astype(BF16)
    m[...] = jnp.broadcast_to(jnp.sum(q2 * _dup(k_ref[0]), axis=1, keepdims=True), m.shape)
    l[...] = jnp.ones(l.shape, F32)
    a[...] = _dup(v_ref[0])


def _head_match(page_rows):
    lane8 = lax.broadcasted_iota(jnp.int32, (2 * N_HEADS, page_rows), 1) & (N_HEADS - 1)
    row8 = lax.broadcasted_iota(jnp.int32, (2 * N_HEADS, page_rows), 0) & (N_HEADS - 1)
    return lane8 == row8


def _dec_update_fox(q16, k_pages, v_pages, l_pages, ux_ref, m_ref, l_ref, a_ref, c_ref, active):
    page_rows = k_pages[0].shape[0]
    n_rep = page_rows // LANE
    valid = _head_match(page_rows)
    q = q16[...]
    ux = ux_ref[...]
    carry = c_ref[...]
    scores = []
    for k_page, l_page in zip(k_pages, l_pages):
        s_all = lax.dot_general(q, k_page[...].astype(BF16), NT_DIMS, preferred_element_type=F32)
        lt = l_page[0]
        b3 = jnp.dot(jnp.concatenate(_split3(lt), axis=0).astype(BF16), ux, preferred_element_type=F32)
        bias = _dup(b3[0:N_HEADS] + b3[N_HEADS:2 * N_HEADS] + b3[2 * N_HEADS:3 * N_HEADS]) + _rep(carry, n_rep)
        scores.append(jnp.where(valid, s_all + bias, NEG_INF))
        carry = carry + _dup(jnp.sum(lt, axis=1, keepdims=True))
    c_ref[...] = jnp.where(active, carry, c_ref[...])
    _dec_online(scores, v_pages, m_ref, l_ref, a_ref, active)


def _dec_update_diff(q16, k_pages, v_pages, m_ref, l_ref, a_ref, active):
    valid = _head_match(k_pages[0].shape[0])
    q = q16[...]
    scores = [jnp.where(valid, lax.dot_general(q, k_page[...].astype(BF16), NT_DIMS, preferred_element_type=F32),
                        NEG_INF) for k_page in k_pages]
    _dec_online(scores, v_pages, m_ref, l_ref, a_ref, active)


def _dec_online(scores, v_pages, m_ref, l_ref, a_ref, active):
    n_rep = scores[0].shape[1] // LANE
    m_prev = m_ref[...]
    m_new = m_prev
    for s in scores:
        m_new = jnp.maximum(m_new, jnp.max(s, axis=1, keepdims=True))
    alpha = jnp.exp(m_prev - m_new)
    m_rep = _rep(m_new, n_rep)
    l_new = alpha * l_ref[...]
    a_new = alpha * a_ref[...]
    for s, v_page in zip(scores, v_pages):
        p = jnp.exp(s - m_rep)
        l_new = l_new + jnp.sum(p, axis=1, keepdims=True)
        a_new = a_new + jnp.dot(p.astype(BF16), v_page[...].astype(BF16), preferred_element_type=F32)
    l_ref[...] = jnp.where(active, l_new, l_ref[...])
    a_ref[...] = jnp.where(active, a_new, a_ref[...])
    m_ref[...] = jnp.where(active, m_new, m_prev)


def _mix_kernel(x_ref, meta_ref, xs_ref, of_ref, od_ref, sf_ref, sd_ref, wt_ref, wb_ref, gi_ref, bi_ref,
                g_ref, b_ref, o_ref, os_ref, *, rows):
    def mixed(x, o_f, o_d):
        y = (ALPHA * _layer_norm(x, gi_ref[...], bi_ref[...])
             + jnp.dot(o_f, wt_ref[...], preferred_element_type=F32)
             + jnp.dot(o_d, wb_ref[...], preferred_element_type=F32))
        return _layer_norm(y, g_ref[...], b_ref[...])

    def body(case):
        o_ref[...] = mixed(_row_space_x(case, x_ref[...], meta_ref, xs_ref, rows), of_ref[...], od_ref[...])
        if case == "last":
            os_ref[...] = mixed(xs_ref[...], sf_ref[...], sd_ref[...])

    _per_tile_case(pl.program_id(0), rows.n_half, body)


def _mix_out(x_prompt, meta, x_sample, o_f, o_d, os_f, os_d, wo_top, wo_bot, gi, bi, g, b, rows):
    D = x_prompt.shape[1]
    tm = rows.tmh
    row = lambda i: (i, 0)
    const = lambda i: (0, 0)
    return pl.pallas_call(
        functools.partial(_mix_kernel, rows=rows),
        grid=(rows.n_half,),
        in_specs=[_x_block_spec(rows, D), pl.BlockSpec(meta.shape, const), pl.BlockSpec(x_sample.shape, const),
                  pl.BlockSpec((tm, GROUP), row), pl.BlockSpec((tm, GROUP), row),
                  pl.BlockSpec((rows.ns, GROUP), const), pl.BlockSpec((rows.ns, GROUP), const),
                  pl.BlockSpec((GROUP, D), const), pl.BlockSpec((GROUP, D), const),
                  pl.BlockSpec((1, D), const), pl.BlockSpec((1, D), const),
                  pl.BlockSpec((1, D), const), pl.BlockSpec((1, D), const)],
        out_specs=[pl.BlockSpec((tm, D), row), pl.BlockSpec((rows.ns, D), const)],
        out_shape=[jax.ShapeDtypeStruct((rows.r, D), F32),
                   jax.ShapeDtypeStruct((rows.ns, D), F32)],
        compiler_params=pltpu.CompilerParams(dimension_semantics=("arbitrary",), vmem_limit_bytes=VMEM_LIMIT),
        name="mix_out_ln1",
    )(x_prompt, meta, x_sample, o_f, o_d, os_f, os_d, wo_top, wo_bot, gi, bi, g, b)


def _ffn_kernel(h_ref, wu_ref, wd_ref, g_ref, b_ref, o_ref, hb_scr, acc_scr):
    j = pl.program_id(1)

    @pl.when(j == 0)
    def _():
        hb_scr[...] = h_ref[...].astype(BF16)
        acc_scr[...] = jnp.zeros_like(acc_scr)

    u = jnp.maximum(jnp.dot(hb_scr[...], wu_ref[...], preferred_element_type=F32), 0.0)
    acc_scr[...] += jnp.dot((u * u).astype(BF16), wd_ref[...], preferred_element_type=F32)

    @pl.when(j == pl.num_programs(1) - 1)
    def _():
        o_ref[...] = _layer_norm(ALPHA * h_ref[...] + acc_scr[...], g_ref[...], b_ref[...])


def _ffn(h1, w_up, w_down, g, b, n_rows, row0, tm, tf):
    D = h1.shape[1]
    d_ff = w_up.shape[1]
    return pl.pallas_call(
        _ffn_kernel,
        grid=(n_rows // tm, d_ff // tf),
        in_specs=[pl.BlockSpec((pl.Element(tm), pl.Element(D)),
                               lambda i, j: (pl.multiple_of(row0 + i * tm, SUBLANE), 0)),
                  pl.BlockSpec((D, tf), lambda i, j: (0, j)),
                  pl.BlockSpec((tf, D), lambda i, j: (j, 0)),
                  pl.BlockSpec((1, D), lambda i, j: (0, 0)),
                  pl.BlockSpec((1, D), lambda i, j: (0, 0))],
        out_specs=pl.BlockSpec((tm, D), lambda i, j: (i, 0)),
        out_shape=jax.ShapeDtypeStruct((n_rows, D), F32),
        scratch_shapes=[pltpu.VMEM((tm, D), BF16), pltpu.VMEM((tm, D), F32)],
        compiler_params=pltpu.CompilerParams(
            dimension_semantics=("arbitrary", "arbitrary"), vmem_limit_bytes=VMEM_LIMIT),
        name="ffn_ln2",
    )(h1, w_up, w_down, g, b)


def _rope_tables(pos):
    half = ROT_DIM // 2
    inv_freq = jnp.power(jnp.float32(ROPE_THETA), -jnp.arange(0, ROT_DIM, 2, dtype=F32) / ROT_DIM)
    ang = pos.astype(F32)[:, None] * inv_freq[None, :]
    cos, sin = jnp.cos(ang), jnp.sin(ang)
    lane = jnp.arange(D_HEAD) % D_MAP
    f = lane % half
    first, second = lane < half, (lane >= half) & (lane < ROT_DIM)
    cos_t = jnp.where((first | second)[None, :], cos[:, f], 1.0)
    sa_t = jnp.where(first[None, :], -sin[:, f], 0.0)
    sb_t = jnp.where(second[None, :], sin[:, f], 0.0)
    return cos_t, sa_t, sb_t


def kernel(x_prompt, x_sample, cache_fox_k, cache_fox_v, cache_fox_logf, cache_diff_k, cache_diff_v, page_table, meta_tokens, ln_in_g, ln_in_b, w_in, b_forget, lambda_q1, lambda_k1, lambda_q2, lambda_k2, diff_norm_g, w_o, ln1_g, ln1_b, w_up, w_down, ln2_g, ln2_b):
    batch, seq, D = x_prompt.shape
    NS, dec_seq, _ = x_sample.shape
    depth, n_pool, page_size = cache_fox_k.shape[:3]
    n_pages = page_table.shape[1]
    assert batch == 1 and dec_seq == 1 and depth == DEPTH
    assert cache_fox_k.shape[3:] == (N_HEADS, D_HEAD) and meta_tokens.shape[0] == N_META
    past_len = n_pages * page_size

    tm = ROW_TILE
    Lr = N_META + seq
    S0 = _round_up(Lr, 32)
    R = _round_up(S0 + NS, tm)
    rows = _Rows(seq=seq, ns=NS, lr=Lr, s0=S0, r=R, tmh=tm // 2)
    assert rows.n_half >= 2 and R - rows.tmh <= Lr and S0 >= R - rows.tmh and seq >= rows.tmh
    xp, xs, meta = x_prompt[0], x_sample[:, 0], meta_tokens.astype(F32)
    pos = jnp.concatenate([jnp.arange(Lr), jnp.zeros((S0 - Lr,), jnp.int32),
                           jnp.full((NS,), past_len), jnp.zeros((R - S0 - NS,), jnp.int32)])
    cos_t, sa_t, sb_t = _rope_tables(pos)

    w_fox = w_in[0].astype(BF16)
    w_diff = w_fox[:, N_GROUPS // 2 * GROUP + N_HEADS:]
    wf = w_in[0][:, N_GROUPS // 2 * GROUP:N_GROUPS // 2 * GROUP + N_HEADS]
    lane = np.arange(LANE)
    bias_lane = (lane >= BIAS_BASE) & (lane < BIAS_BASE + BIAS_W * N_HEADS)
    head_of_lane = np.where(bias_lane, (lane - BIAS_BASE) // BIAS_W, lane % N_HEADS)
    used = bias_lane | (lane < N_HEADS)
    w_f = jnp.where(used[None, :], wf[:, head_of_lane], 0.0).astype(BF16)
    b_f = jnp.where(used, b_forget[0].astype(F32)[head_of_lane], 0.0)[None, :]
    lane_j = jnp.asarray(np.where(bias_lane, (lane - BIAS_BASE) % BIAS_W, -1)[None, :], jnp.int32)
    row2 = lambda v: v.astype(F32).reshape(1, -1)
    gi, bi = row2(ln_in_g), row2(ln_in_b)

    hb, qb, kb, lf = _ln_gate(xp, meta, xs, gi, bi, w_f, b_f, lane_j, rows)
    zb, p_fk, p_fv, p_dk, p_dv, s_fk, s_fv, s_dk, s_dv = _project(hb, w_fox, w_diff, cos_t, sa_t, sb_t,
                                                                  tm, Lr, S0, NS)

    lamp = jnp.zeros((SUBLANE, LANE), F32).at[0:4, 0:D_MAP].set(
        jnp.stack([lambda_q1[0], lambda_k1[0], lambda_q2[0], lambda_k2[0]]).astype(F32))
    gn = row2(diff_norm_g[0])

    grp = lambda g: zb[S0:S0 + NS, g * GROUP:(g + 1) * GROUP].reshape(NS, N_HEADS, D_HEAD).astype(F32)
    lf_own = jnp.broadcast_to(lf[S0:S0 + NS, :, None], (NS, N_HEADS, LANE))
    page_rows = page_size * N_HEADS
    flat = lambda c: c[0].reshape(n_pool * page_rows, D_HEAD)
    lf_cache_t = jnp.swapaxes(cache_fox_logf[0], 1, 2)
    s_idx = jnp.arange(page_size)
    ux = (s_idx[:, None] > jnp.repeat(s_idx, N_HEADS)[None, :]).astype(BF16)
    qi_tab, ki_tab = _causal_steps(R // tm)
    st = _stream_plan(NS, n_pages, qi_tab.shape[0], page_rows)
    tables = (qi_tab, ki_tab, _step_pages(page_table, st))
    o_f, os_d = _fox_attention(zb, qb, kb, tables, st, tm, (grp(3), grp(4), grp(5)), lamp, gn,
                               flat(cache_diff_k), flat(cache_diff_v))
    o_d, os_f = _diff_attention(zb, lamp, gn, tables, st, tm, (grp(0), grp(1), grp(2), lf_own), ux,
                                flat(cache_fox_k), flat(cache_fox_v), lf_cache_t)

    wo = w_o[0].astype(BF16)
    h1, h1_s = _mix_out(xp, meta, xs, o_f, o_d, os_f.reshape(NS, GROUP).astype(BF16),
                        os_d.reshape(NS, GROUP).astype(BF16), wo[:GROUP], wo[GROUP:], gi, bi,
                        row2(ln1_g[0]), row2(ln1_b[0]), rows)
    ffn = functools.partial(_ffn, w_up=w_up[0].astype(BF16), w_down=w_down[0].astype(BF16),
                            g=row2(ln2_g[0]), b=row2(ln2_b[0]), tf=1024)
    tm_ffn = next(t for t in (512, 256, 128) if seq % t == 0)
    y_prompt = ffn(h1, n_rows=seq, row0=N_META, tm=tm_ffn)
    y_sample = ffn(h1_s, n_rows=NS, row0=0, tm=NS)

    prompt_cache = lambda a: a.reshape(1, 1, Lr, N_HEADS, D_HEAD)
    sample_cache = lambda a: a.reshape(1, NS, 1, N_HEADS, D_HEAD)
    return (y_prompt[None], y_sample[:, None],
            prompt_cache(p_fk), prompt_cache(p_fv), lf[:Lr][None, None], prompt_cache(p_dk), prompt_cache(p_dv),
            sample_cache(s_fk), sample_cache(s_fv), lf[S0:S0 + NS][None, :, None],
            sample_cache(s_dk), sample_cache(s_dv))
```

```python
import functools
import math
from typing import NamedTuple

import jax
import jax.numpy as jnp
import numpy as np
from jax import lax
from jax.experimental import pallas as pl
from jax.experimental.pallas import tpu as pltpu

F32 = jnp.float32
BF16 = jnp.bfloat16

N_META = 16
N_HEADS = 8
D_HEAD = 128
D_MAP = 64
ROT_DIM = D_MAP // 4
ROPE_THETA = 500000.0
DEPTH = 1
ALPHA = (2 * DEPTH) ** 0.25
LN_EPS = 1e-5
NEG_INF = -1e30
LAM_INIT = 0.8 - 0.6 * math.exp(-0.3 * 0)
FOX_SCALE = D_HEAD ** -0.5
DIFF_SCALE = D_MAP ** -0.5
GROUP = N_HEADS * D_HEAD
N_GROUPS = 6
BIAS_BASE = N_HEADS
BIAS_W = 6

LANE = 128
SUBLANE = 8
MXU_DEPTH = 256
VMEM_LIMIT = 58 * 1024 * 1024
HEAD_GROUPS = 2
HEADS_PER_GROUP = N_HEADS // HEAD_GROUPS
GROUP_W = HEADS_PER_GROUP * D_HEAD
DIAG_BLOCKS = 3
ROW_TILE = DIAG_BLOCKS * MXU_DEPTH

NT_DIMS = (((1,), (1,)), ((), ()))


def _round_up(x, m):
    return (x + m - 1) // m * m


def _layer_norm(x, g, b):
    mu = jnp.mean(x, -1, keepdims=True)
    xc = x - mu
    var = jnp.mean(xc * xc, -1, keepdims=True)
    return xc * lax.rsqrt(var + LN_EPS) * g + b


def _split3(x):
    hi = x.astype(BF16).astype(F32)
    r1 = x - hi
    mid = r1.astype(BF16).astype(F32)
    lo = (r1 - mid).astype(BF16).astype(F32)
    return hi, mid, lo


def _rep(x, n):
    return jnp.concatenate([x] * n, axis=1)


class _Rows(NamedTuple):
    seq: int
    ns: int
    lr: int
    s0: int
    r: int
    tmh: int

    @property
    def n_half(self):
        return self.r // self.tmh


def _x_block_spec(rows, d):
    def index(i, *_):
        return pl.multiple_of(jnp.clip(i * rows.tmh - N_META, 0, rows.seq - rows.tmh), SUBLANE), 0
    return pl.BlockSpec((pl.Element(rows.tmh), pl.Element(d)), index)


def _row_space_x(case, blk, meta_ref, xs_ref, rows):
    if case == "first":
        return jnp.concatenate([meta_ref[...], blk[0:rows.tmh - N_META]], axis=0)
    if case == "mid":
        return blk
    n_prompt = rows.lr - (rows.r - rows.tmh)
    parts = [blk[rows.tmh - n_prompt:rows.tmh]]
    for n, val in ((rows.s0 - rows.lr, None), (rows.ns, xs_ref), (rows.r - rows.s0 - rows.ns, None)):
        if n:
            parts.append(jnp.zeros((n, blk.shape[1]), F32) if val is None else val[...])
    return jnp.concatenate(parts, axis=0)


def _per_tile_case(i, n, body):
    pl.when(i == 0)(functools.partial(body, "first"))
    pl.when((i > 0) & (i < n - 1))(functools.partial(body, "mid"))
    pl.when(i == n - 1)(functools.partial(body, "last"))


def _gate_kernel(x_ref, meta_ref, xs_ref, g_ref, b_ref, wf_ref, bf_ref, lanej_ref,
                 hb_ref, qb_ref, kb_ref, lf_ref, carry_scr, *, rows):
    i = pl.program_id(0)
    tm = rows.tmh

    @pl.when(i == 0)
    def _():
        carry_scr[...] = jnp.zeros_like(carry_scr)

    def body(case):
        x = _row_space_x(case, x_ref[...], meta_ref, xs_ref, rows)
        hb = _layer_norm(x, g_ref[...], b_ref[...]).astype(BF16)
        hb_ref[...] = hb
        ff = jnp.dot(hb, wf_ref[...], preferred_element_type=F32) + bf_ref[...]
        lf = jnp.minimum(ff, 0.0) - jnp.log1p(jnp.exp(-jnp.abs(ff)))
        lf_ref[...] = lf[:, 0:N_HEADS]
        row = lax.broadcasted_iota(jnp.int32, (tm, tm), 0)
        col = lax.broadcasted_iota(jnp.int32, (tm, tm), 1)
        tri = (col <= row).astype(F32).astype(BF16)
        pieces = jnp.concatenate(_split3(lf), axis=1).astype(BF16)
        c3 = jnp.dot(tri, pieces, preferred_element_type=F32)
        cs = c3[:, 0:LANE] + c3[:, LANE:2 * LANE] + c3[:, 2 * LANE:3 * LANE] + carry_scr[0:1, :]
        carry_scr[...] = jnp.broadcast_to(cs[tm - 1:tm, :], carry_scr.shape)
        hi, mid, lo = _split3(cs)
        j = jnp.broadcast_to(lanej_ref[...], (tm, LANE))
        one = jnp.ones((tm, LANE), F32)
        zero = jnp.zeros((tm, LANE), F32)
        qb_ref[...] = jnp.where(j == 0, hi, jnp.where(j == 1, mid, jnp.where(
            j == 2, lo, jnp.where(j >= 3, one, zero)))).astype(BF16)
        kb_ref[...] = jnp.where(j == 3, -hi, jnp.where(j == 4, -mid, jnp.where(
            j == 5, -lo, jnp.where(j >= 0, one, zero)))).astype(BF16)

    _per_tile_case(i, rows.n_half, body)


def _ln_gate(x_prompt, meta, x_sample, ln_g, ln_b, w_f, b_f, lane_j, rows):
    D = x_prompt.shape[1]
    tm = rows.tmh
    row = lambda i: (i, 0)
    const = lambda i: (0, 0)
    return pl.pallas_call(
        functools.partial(_gate_kernel, rows=rows),
        grid=(rows.n_half,),
        in_specs=[_x_block_spec(rows, D), pl.BlockSpec(meta.shape, const), pl.BlockSpec(x_sample.shape, const),
                  pl.BlockSpec((1, D), const), pl.BlockSpec((1, D), const),
                  pl.BlockSpec((D, LANE), const), pl.BlockSpec((1, LANE), const), pl.BlockSpec((1, LANE), const)],
        out_specs=[pl.BlockSpec((tm, D), row), pl.BlockSpec((tm, LANE), row),
                   pl.BlockSpec((tm, LANE), row), pl.BlockSpec((tm, N_HEADS), row)],
        out_shape=[jax.ShapeDtypeStruct((rows.r, D), BF16),
                   jax.ShapeDtypeStruct((rows.r, LANE), BF16),
                   jax.ShapeDtypeStruct((rows.r, LANE), BF16),
                   jax.ShapeDtypeStruct((rows.r, N_HEADS), F32)],
        scratch_shapes=[pltpu.VMEM((SUBLANE, LANE), F32)],
        compiler_params=pltpu.CompilerParams(dimension_semantics=("arbitrary",), vmem_limit_bytes=VMEM_LIMIT),
        name="ln_gate",
    )(x_prompt, meta, x_sample, ln_g, ln_b, w_f, b_f, lane_j)


def _rope(zh, cos, sa, sb):
    return zh * cos + pltpu.roll(zh, D_HEAD - ROT_DIM // 2, 1) * sa + pltpu.roll(zh, ROT_DIM // 2, 1) * sb


def _proj_kernel(hb_ref, wfox_ref, wdiff_ref, cos_ref, sa_ref, sb_ref, zb_ref, pk_ref, pv_ref, pdk_ref, pdv_ref,
                 sk_ref, sv_ref, sdk_ref, sdv_ref, *, tm, s_tile, s_off, ns):
    j = pl.program_id(0)
    i = pl.program_id(1)
    heads = [slice(h * D_HEAD, (h + 1) * D_HEAD) for h in range(N_HEADS)]

    def matmul(w_ref):
        return jnp.dot(hb_ref[...], w_ref[...], preferred_element_type=F32)

    def cache_store(per_head, p_ref, s_ref):
        for h in range(N_HEADS):
            p_ref[pl.ds(h, tm, stride=N_HEADS), :] = per_head[h]

        @pl.when(i == s_tile)
        def _():
            for h in range(N_HEADS):
                s_ref[pl.ds(h, ns, stride=N_HEADS), :] = per_head[h][s_off:s_off + ns]

    @pl.when(j == 0)
    def _():
        zb_ref[...] = (matmul(wfox_ref) * FOX_SCALE).astype(BF16)

    def plain(w_ref, p_ref, s_ref):
        z = matmul(w_ref)
        zb_ref[...] = z.astype(BF16)
        cache_store([z[:, hs] for hs in heads], p_ref, s_ref)

    pl.when(j == 1)(functools.partial(plain, wfox_ref, pk_ref, sk_ref))
    pl.when(j == 2)(functools.partial(plain, wfox_ref, pv_ref, sv_ref))
    pl.when(j == 5)(functools.partial(plain, wdiff_ref, pdv_ref, sdv_ref))

    @pl.when(j == 3)
    def _():
        z = matmul(wdiff_ref)
        cos, sa, sb = cos_ref[...], sa_ref[...], sb_ref[...]
        for hs in heads:
            zb_ref[:, hs] = (_rope(z[:, hs], cos, sa, sb) * DIFF_SCALE).astype(BF16)

    @pl.when(j == 4)
    def _():
        z = matmul(wdiff_ref)
        cos, sa, sb = cos_ref[...], sa_ref[...], sb_ref[...]
        ys = [_rope(z[:, hs], cos, sa, sb) for hs in heads]
        for hs, y in zip(heads, ys):
            zb_ref[:, hs] = y.astype(BF16)
        cache_store(ys, pdk_ref, sdk_ref)


def _project(hb, w_fox, w_diff, cos_t, sa_t, sb_t, tm, n_prompt_rows, s0, ns):
    R, D = hb.shape
    n = R // tm
    half = N_GROUPS // 2
    row = lambda j, i: (i, 0)

    def cache_spec(group):
        return pl.BlockSpec((tm * N_HEADS, D_HEAD),
                            lambda j, i: (jnp.where(j < group, 0, jnp.where(j == group, i, n - 1)), 0))

    sample_spec = pl.BlockSpec((ns * N_HEADS, D_HEAD), lambda j, i: (0, 0))
    cache_shape = jax.ShapeDtypeStruct((n_prompt_rows * N_HEADS, D_HEAD), F32)
    sample_shape = jax.ShapeDtypeStruct((ns * N_HEADS, D_HEAD), F32)
    return pl.pallas_call(
        functools.partial(_proj_kernel, tm=tm, s_tile=s0 // tm, s_off=s0 % tm, ns=ns),
        grid=(N_GROUPS, n),
        in_specs=[pl.BlockSpec((tm, D), row),
                  pl.BlockSpec((D, GROUP), lambda j, i: (0, jnp.minimum(j, half - 1))),
                  pl.BlockSpec((D, GROUP), lambda j, i: (0, jnp.maximum(j - half, 0))),
                  pl.BlockSpec((tm, LANE), row), pl.BlockSpec((tm, LANE), row), pl.BlockSpec((tm, LANE), row)],
        out_specs=[pl.BlockSpec((tm, GROUP), lambda j, i: (i, j)),
                   cache_spec(1), cache_spec(2), cache_spec(4), cache_spec(5),
                   sample_spec, sample_spec, sample_spec, sample_spec],
        out_shape=[jax.ShapeDtypeStruct((R, N_GROUPS * GROUP), BF16),
                   cache_shape, cache_shape, cache_shape, cache_shape,
                   sample_shape, sample_shape, sample_shape, sample_shape],
        compiler_params=pltpu.CompilerParams(
            dimension_semantics=("arbitrary", "arbitrary"), vmem_limit_bytes=VMEM_LIMIT),
        name="in_proj",
    )(hb, w_fox, w_diff, cos_t, sa_t, sb_t)


def _flash_block(qs_scr, k_ref, kb_ref, v_ref, m_scr, acc_scr, h, r0, nr, nc, keep):
    hs = slice(h * D_HEAD, (h + 1) * D_HEAD)
    kh = k_ref[0:nc, hs]
    if kb_ref is not None:
        kh = jnp.concatenate([kh, kb_ref[0:nc, :]], axis=1)
    s = lax.dot_general(qs_scr[h, r0:r0 + nr, :], kh, NT_DIMS, preferred_element_type=F32)
    if keep is not None:
        s = jnp.where(keep, s, NEG_INF)
    m_prev = m_scr[h, r0:r0 + nr, :]
    m_new = jnp.maximum(m_prev, jnp.max(s, axis=1, keepdims=True))
    p = jnp.exp(s - _rep(m_new, nc // LANE))
    alpha = jnp.exp(m_prev - m_new)
    vh = jnp.concatenate([v_ref[0:nc, hs], jnp.ones((nc, D_HEAD), BF16)], axis=1)
    pv = jnp.dot(p.astype(BF16), vh, preferred_element_type=F32)
    acc_scr[h, r0:r0 + nr, :] = acc_scr[h, r0:r0 + nr, :] * _rep(alpha, 2) + pv
    m_scr[h, r0:r0 + nr, :] = m_new


def _flash_full(qs_scr, k_ref, kb_ref, v_ref, m_scr, acc_scr, *, tq, nmap):
    for h in range(HEADS_PER_GROUP):
        _flash_block(qs_scr, k_ref, kb_ref, v_ref, m_scr, acc_scr, h, 0, nmap * tq, tq, None)


def _flash_diag(qs_scr, k_ref, kb_ref, v_ref, m_scr, acc_scr, *, tq, nmap):
    sub = tq // DIAG_BLOCKS
    for h in range(HEADS_PER_GROUP):
        for rb in range(DIAG_BLOCKS):
            nc = (rb + 1) * sub
            row = lax.broadcasted_iota(jnp.int32, (sub, nc), 0) + rb * sub
            keep = lax.broadcasted_iota(jnp.int32, (sub, nc), 1) <= row
            for mp in range(nmap):
                _flash_block(qs_scr, k_ref, kb_ref, v_ref, m_scr, acc_scr, h, mp * tq + rb * sub, sub, nc, keep)


def _flash_init(m_scr, acc_scr):
    m_scr[...] = jnp.full(m_scr.shape, NEG_INF, F32)
    acc_scr[...] = jnp.zeros(acc_scr.shape, F32)


def _fox_kernel(qi_tab, ki_tab, pt_ref, q_ref, k_ref, v_ref, qb_ref, kb_ref,
                dq_ref, dk_ref, dv_ref, lamp_ref, gn_ref, *rest, tq, st):
    n = st.pps
    k_pages, v_pages = rest[:n], rest[n:2 * n]
    o_ref, os_ref, qs_scr, m_scr, acc_scr, q16, dm, dl, da = rest[2 * n:]
    g = pl.program_id(0)
    s = pl.program_id(1)
    qi = qi_tab[s]
    ki = ki_tab[s]
    _, _, c, active = _stream_pos(g, s, st)

    @pl.when(active & (c == 0))
    def _():
        _dec_init_diff(dq_ref, dk_ref, dv_ref, q16, dm, dl, da)

    @pl.when(ki == 0)
    def _():
        _flash_init(m_scr, acc_scr)
        qb = qb_ref[...].astype(F32)
        lane = lax.broadcasted_iota(jnp.int32, qb.shape, 1)
        for h in range(HEADS_PER_GROUP):
            lo = BIAS_BASE + BIAS_W * (g * HEADS_PER_GROUP + h)
            qbh = jnp.where((lane >= lo) & (lane < lo + BIAS_W), qb, 0.0).astype(BF16)
            qs_scr[h] = jnp.concatenate([q_ref[:, h * D_HEAD:(h + 1) * D_HEAD], qbh], axis=1)

    args = (qs_scr, k_ref, kb_ref, v_ref, m_scr, acc_scr)
    decode = functools.partial(_dec_update_diff, q16, k_pages, v_pages, dm, dl, da, active)

    @pl.when(ki != qi)
    def _():
        decode()
        _flash_full(*args, tq=tq, nmap=1)

    @pl.when(ki == qi)
    def _():
        decode()
        _flash_diag(*args, tq=tq, nmap=1)
        for h in range(HEADS_PER_GROUP):
            a = acc_scr[h]
            o_ref[:, h * D_HEAD:(h + 1) * D_HEAD] = (a[:, :D_HEAD] / a[:, D_HEAD:]).astype(o_ref.dtype)

    @pl.when(active & (c == st.cps - 1))
    def _():
        o = da[...] / dl[...]
        os_ref[0] = _diff_finish(o[0:N_HEADS], o[N_HEADS:], _diff_lambda(lamp_ref), gn_ref[...])


def _diff_lambda(lamp_ref):
    lp = lamp_ref[...]
    return (jnp.exp(jnp.sum(lp[0:1] * lp[1:2], axis=1, keepdims=True))
            - jnp.exp(jnp.sum(lp[2:3] * lp[3:4], axis=1, keepdims=True)) + LAM_INIT)


def _diff_finish(o1, o2, lam, gn):
    od = o1 - lam * o2
    od = od * lax.rsqrt(jnp.mean(od * od, axis=-1, keepdims=True) + LN_EPS)
    return od * gn * (1.0 - LAM_INIT)


def _diff_kernel(qi_tab, ki_tab, pt_ref, q_ref, k_ref, v_ref, lamp_ref, gn_ref,
                 fq_ref, fk_ref, fv_ref, lfo_ref, *rest, tq, st):
    n = st.pps
    k_pages, v_pages, l_pages = rest[:n], rest[n:2 * n], rest[2 * n:3 * n]
    o_ref, os_ref, qs_scr, m_scr, acc_scr, q16, dm, dl, da, dc = rest[3 * n:]
    g = pl.program_id(0)
    s = pl.program_id(1)
    qi = qi_tab[s]
    ki = ki_tab[s]
    _, _, c, active = _stream_pos(g, s, st)

    @pl.when(active & (c == 0))
    def _():
        _dec_init_fox(fq_ref, fk_ref, fv_ref, lfo_ref, q16, dm, dl, da, dc)

    @pl.when(ki == 0)
    def _():
        _flash_init(m_scr, acc_scr)
        lane = lax.broadcasted_iota(jnp.int32, (tq, D_HEAD), 1)
        for h in range(HEADS_PER_GROUP):
            qh = q_ref[:, h * D_HEAD:(h + 1) * D_HEAD].astype(F32)
            qs_scr[h, 0:tq, :] = jnp.where(lane < D_MAP, qh, 0.0).astype(BF16)
            qs_scr[h, tq:2 * tq, :] = jnp.where(lane >= D_MAP, qh, 0.0).astype(BF16)

    args = (qs_scr, k_ref, None, v_ref, m_scr, acc_scr)
    decode = functools.partial(_dec_update_fox, q16, k_pages, v_pages, l_pages, dm, dl, da, dc, active)

    @pl.when(ki != qi)
    def _():
        decode()
        _flash_full(*args, tq=tq, nmap=2)

    @pl.when(ki == qi)
    def _():
        decode()
        _flash_diag(*args, tq=tq, nmap=2)
        lam = _diff_lambda(lamp_ref)
        gn = gn_ref[...]
        for h in range(HEADS_PER_GROUP):
            a = acc_scr[h]
            o1 = a[0:tq, :D_HEAD] / a[0:tq, D_HEAD:]
            o2 = a[tq:2 * tq, :D_HEAD] / a[tq:2 * tq, D_HEAD:]
            o_ref[:, h * D_HEAD:(h + 1) * D_HEAD] = _diff_finish(o1, o2, lam, gn).astype(o_ref.dtype)

    @pl.when(active & (c == st.cps - 1))
    def _():
        os_ref[0] = (da[...] / dl[...])[0:N_HEADS]


def _causal_steps(n_tiles):
    qi = [q for q in range(n_tiles) for _ in range(q + 1)]
    ki = [k for q in range(n_tiles) for k in range(q + 1)]
    return jnp.asarray(qi, jnp.int32), jnp.asarray(ki, jnp.int32)


def _flash_specs(first_group, tq):
    def spec(grp, by_key_tile):
        return pl.BlockSpec((tq, GROUP_W),
                            lambda g, s, qt, kt, pt: ((kt if by_key_tile else qt)[s], grp * HEAD_GROUPS + g))
    return [spec(first_group, False), spec(first_group + 1, True), spec(first_group + 2, True)]


def _attention_call(kernel_fn, name, st, tq, R, tables, flash_in, flash_specs, seq_in, const_in, const_specs,
                    page_arrays, n_maps, q_width, n_state):
    qi_tab, ki_tab, page_table = tables
    per_seq = pl.BlockSpec((1, N_HEADS, D_HEAD), lambda g, s, qt, kt, pt: (_stream_pos(g, s, st)[1], 0, 0))
    page_specs = []
    for arr in page_arrays:
        page_specs += [_page_spec(arr, r, st) for r in range(st.pps)]
    page_args = [arr for arr in page_arrays for _ in range(st.pps)]
    grid_spec = pltpu.PrefetchScalarGridSpec(
        num_scalar_prefetch=3,
        grid=(HEAD_GROUPS, qi_tab.shape[0]),
        in_specs=flash_specs + [per_seq] * len(seq_in) + const_specs + page_specs,
        out_specs=[pl.BlockSpec((tq, GROUP_W), lambda g, s, qt, kt, pt: (qt[s], g)), per_seq],
        scratch_shapes=(
            [pltpu.VMEM((HEADS_PER_GROUP, n_maps * tq, q_width), BF16),
             pltpu.VMEM((HEADS_PER_GROUP, n_maps * tq, LANE), F32),
             pltpu.VMEM((HEADS_PER_GROUP, n_maps * tq, 2 * D_HEAD), F32),
             pltpu.VMEM((2 * N_HEADS, D_HEAD), BF16)]
            + [pltpu.VMEM((2 * N_HEADS, LANE), F32)] * n_state),
    )
    return pl.pallas_call(
        functools.partial(kernel_fn, tq=tq, st=st),
        grid_spec=grid_spec,
        out_shape=[jax.ShapeDtypeStruct((R, GROUP), BF16),
                   jax.ShapeDtypeStruct((st.ns, N_HEADS, D_HEAD), F32)],
        compiler_params=pltpu.CompilerParams(
            dimension_semantics=("arbitrary", "arbitrary"), vmem_limit_bytes=VMEM_LIMIT),
        name=name,
    )(qi_tab, ki_tab, page_table, *flash_in, *seq_in, *const_in, *page_args)


def _fox_attention(zb, qb, kb, tables, st, tq, dec_seq, lamp, gn, dk_cache, dv_cache):
    const = lambda g, s, qt, kt, pt: (0, 0)
    flash_specs = _flash_specs(0, tq) + [pl.BlockSpec((tq, LANE), lambda g, s, qt, kt, pt: (qt[s], 0)),
                                         pl.BlockSpec((tq, LANE), lambda g, s, qt, kt, pt: (kt[s], 0))]
    return _attention_call(_fox_kernel, "fox_flash", st, tq, zb.shape[0], tables,
                           [zb, zb, zb, qb, kb], flash_specs, dec_seq, [lamp, gn],
                           [pl.BlockSpec((SUBLANE, LANE), const), pl.BlockSpec((1, LANE), const)],
                           [dk_cache, dv_cache], n_maps=1, q_width=2 * D_HEAD, n_state=3)


def _diff_attention(zb, lamp, gn, tables, st, tq, dec_seq, k_cache, v_cache, bias_cache):
    const = lambda g, s, qt, kt, pt: (0, 0)
    flash_specs = _flash_specs(N_GROUPS // 2, tq) + [pl.BlockSpec((SUBLANE, LANE), const),
                                                     pl.BlockSpec((1, LANE), const)]
    return _attention_call(_diff_kernel, "diff_flash", st, tq, zb.shape[0], tables,
                           [zb, zb, zb, lamp, gn], flash_specs, dec_seq, [], [],
                           [k_cache, v_cache, bias_cache], n_maps=2, q_width=D_HEAD, n_state=4)


class _Stream(NamedTuple):
    ns: int
    n_pages: int
    cps: int
    pps: int
    n_pairs: int
    page_rows: int


def _stream_plan(ns, n_pages, n_pairs, page_rows):
    n_steps = HEAD_GROUPS * n_pairs
    fits = [c for c in (8, 4, 2, 1) if ns * c <= n_steps and n_pages % c == 0]
    assert fits, "the attention grid has fewer steps than decode sequences"
    return _Stream(ns=ns, n_pages=n_pages, cps=fits[0], pps=n_pages // fits[0], n_pairs=n_pairs,
                   page_rows=page_rows)


def _step_pages(page_table, st):
    t = np.arange(HEAD_GROUPS * st.n_pairs)
    b = np.minimum(t // st.cps, st.ns - 1)
    cols = st.n_pages - 1 - ((t % st.cps)[:, None] * st.pps + np.arange(st.pps)[None, :])
    return page_table[b[:, None], cols].reshape(-1)


def _stream_pos(g, s, st):
    t = g * st.n_pairs + s
    shift = st.cps.bit_length() - 1
    return t, jnp.minimum(t >> shift, st.ns - 1), t & (st.cps - 1), t < st.ns * st.cps


def _page_spec(cache, r, st):
    def page(g, s, qt, kt, pt):
        return pt[(g * st.n_pairs + s) * st.pps + r]
    if cache.shape[1] != D_HEAD:
        return pl.BlockSpec((N_HEADS, cache.shape[1]), lambda *a: (page(*a), 0))
    return pl.BlockSpec((st.page_rows, D_HEAD), lambda *a: (page(*a), 0))


def _page_bias_kernel(l_ref, ux_ref, o_ref):
    lt = l_ref[...]
    n = lt.shape[0] * N_HEADS
    lt = lt.reshape(n, lt.shape[2])
    b3 = jnp.dot(jnp.concatenate(_split3(lt), axis=0).astype(BF16), ux_ref[...], preferred_element_type=F32)
    flat = ux_ref.shape[1]
    o_ref[:, 0:flat] = b3[0:n] + b3[n:2 * n] + b3[2 * n:3 * n]
    o_ref[:, flat:] = lt


def _page_bias(lf_cache_t, ux):
    n_pool, _, page = lf_cache_t.shape
    flat = ux.shape[1]
    g = next(c for c in (64, 32, 16, 8, 4, 2, 1) if n_pool % c == 0)
    return pl.pallas_call(
        _page_bias_kernel,
        grid=(n_pool // g,),
        in_specs=[pl.BlockSpec((g, N_HEADS, page), lambda i: (i, 0, 0)), pl.BlockSpec(ux.shape, lambda i: (0, 0))],
        out_specs=pl.BlockSpec((g * N_HEADS, flat + page), lambda i: (i, 0)),
        out_shape=jax.ShapeDtypeStruct((n_pool * N_HEADS, flat + page), F32),
        compiler_params=pltpu.CompilerParams(dimension_semantics=("arbitrary",), vmem_limit_bytes=VMEM_LIMIT),
        name="page_bias",
    )(lf_cache_t, ux)


def _dup(x):
    return jnp.concatenate([x, x], axis=0)


def _dec_init_fox(q_ref, k_ref, v_ref, lfo_ref, q16, m, l, a, c):
    q2 = _dup(q_ref[0])
    q16[...] = q2.astype(BF16)
    m[...] = jnp.broadcast_to(jnp.sum(q2 * _dup(k_ref[0]), axis=1, keepdims=True), m.shape)
    l[...] = jnp.ones(l.shape, F32)
    a[...] = _dup(v_ref[0])
    c[...] = _dup(lfo_ref[0])


def _dec_init_diff(q_ref, k_ref, v_ref, q16, m, l, a):
    lane = lax.broadcasted_iota(jnp.int32, (N_HEADS, D_HEAD), 1)
    qd = q_ref[0]
    q2 = jnp.concatenate([jnp.where(lane < D_MAP, qd, 0.0), jnp.where(lane >= D_MAP, qd, 0.0)], axis=0)
    q16[...] = q2.astype(BF16)
    m[...] = jnp.broadcast_to(jnp.sum(q2 * _dup(k_ref[0]), axis=1, keepdims=True), m.shape)
    l[...] = jnp.ones(l.shape, F32)
    a[...] = _dup(v_ref[0])


def _head_match(page_rows):
    lane8 = lax.broadcasted_iota(jnp.int32, (2 * N_HEADS, page_rows), 1) & (N_HEADS - 1)
    row8 = lax.broadcasted_iota(jnp.int32, (2 * N_HEADS, page_rows), 0) & (N_HEADS - 1)
    return lane8 == row8


def _dec_update_fox(q16, k_pages, v_pages, l_pages, m_ref, l_ref, a_ref, c_ref, active):
    page_rows = k_pages[0].shape[0]
    n_rep = page_rows // LANE
    valid = _head_match(page_rows)
    q = q16[...]
    carry = c_ref[...]
    scores = []
    for k_page, l_page in zip(k_pages, l_pages):
        s_all = lax.dot_general(q, k_page[...].astype(BF16), NT_DIMS, preferred_element_type=F32)
        bias = _dup(l_page[:, 0:page_rows]) + _rep(carry, n_rep)
        scores.append(jnp.where(valid, s_all + bias, NEG_INF))
        carry = carry + _dup(jnp.sum(l_page[:, page_rows:], axis=1, keepdims=True))
    c_ref[...] = jnp.where(active, carry, c_ref[...])
    _dec_online(scores, v_pages, m_ref, l_ref, a_ref, active)


def _dec_update_diff(q16, k_pages, v_pages, m_ref, l_ref, a_ref, active):
    valid = _head_match(k_pages[0].shape[0])
    q = q16[...]
    scores = [jnp.where(valid, lax.dot_general(q, k_page[...].astype(BF16), NT_DIMS, preferred_element_type=F32),
                        NEG_INF) for k_page in k_pages]
    _dec_online(scores, v_pages, m_ref, l_ref, a_ref, active)


def _dec_online(scores, v_pages, m_ref, l_ref, a_ref, active):
    n_rep = scores[0].shape[1] // LANE
    m_prev = m_ref[...]
    m_new = m_prev
    for s in scores:
        m_new = jnp.maximum(m_new, jnp.max(s, axis=1, keepdims=True))
    alpha = jnp.exp(m_prev - m_new)
    m_rep = _rep(m_new, n_rep)
    l_new = alpha * l_ref[...]
    a_new = alpha * a_ref[...]
    for s, v_page in zip(scores, v_pages):
        p = jnp.exp(s - m_rep)
        l_new = l_new + jnp.sum(p, axis=1, keepdims=True)
        a_new = a_new + jnp.dot(p.astype(BF16), v_page[...].astype(BF16), preferred_element_type=F32)
    l_ref[...] = jnp.where(active, l_new, l_ref[...])
    a_ref[...] = jnp.where(active, a_new, a_ref[...])
    m_ref[...] = jnp.where(active, m_new, m_prev)


def _mix_kernel(x_ref, meta_ref, xs_ref, of_ref, od_ref, sf_ref, sd_ref, wt_ref, wb_ref, gi_ref, bi_ref,
                g_ref, b_ref, o_ref, os_ref, *, rows):
    def mixed(x, o_f, o_d):
        y = (ALPHA * _layer_norm(x, gi_ref[...], bi_ref[...])
             + jnp.dot(o_f, wt_ref[...], preferred_element_type=F32)
             + jnp.dot(o_d, wb_ref[...], preferred_element_type=F32))
        return _layer_norm(y, g_ref[...], b_ref[...])

    def body(case):
        o_ref[...] = mixed(_row_space_x(case, x_ref[...], meta_ref, xs_ref, rows), of_ref[...], od_ref[...])
        if case == "last":
            os_ref[...] = mixed(xs_ref[...], sf_ref[...], sd_ref[...])

    _per_tile_case(pl.program_id(0), rows.n_half, body)


def _mix_out(x_prompt, meta, x_sample, o_f, o_d, os_f, os_d, wo_top, wo_bot, gi, bi, g, b, rows):
    D = x_prompt.shape[1]
    tm = rows.tmh
    row = lambda i: (i, 0)
    const = lambda i: (0, 0)
    return pl.pallas_call(
        functools.partial(_mix_kernel, rows=rows),
        grid=(rows.n_half,),
        in_specs=[_x_block_spec(rows, D), pl.BlockSpec(meta.shape, const), pl.BlockSpec(x_sample.shape, const),
                  pl.BlockSpec((tm, GROUP), row), pl.BlockSpec((tm, GROUP), row),
                  pl.BlockSpec((rows.ns, GROUP), const), pl.BlockSpec((rows.ns, GROUP), const),
                  pl.BlockSpec((GROUP, D), const), pl.BlockSpec((GROUP, D), const),
                  pl.BlockSpec((1, D), const), pl.BlockSpec((1, D), const),
                  pl.BlockSpec((1, D), const), pl.BlockSpec((1, D), const)],
        out_specs=[pl.BlockSpec((tm, D), row), pl.BlockSpec((rows.ns, D), const)],
        out_shape=[jax.ShapeDtypeStruct((rows.r, D), F32),
                   jax.ShapeDtypeStruct((rows.ns, D), F32)],
        compiler_params=pltpu.CompilerParams(dimension_semantics=("arbitrary",), vmem_limit_bytes=VMEM_LIMIT),
        name="mix_out_ln1",
    )(x_prompt, meta, x_sample, o_f, o_d, os_f, os_d, wo_top, wo_bot, gi, bi, g, b)


def _ffn_kernel(h_ref, wu_ref, wd_ref, g_ref, b_ref, o_ref, hb_scr, acc_scr):
    j = pl.program_id(1)

    @pl.when(j == 0)
    def _():
        hb_scr[...] = h_ref[...].astype(BF16)
        acc_scr[...] = jnp.zeros_like(acc_scr)

    u = jnp.maximum(jnp.dot(hb_scr[...], wu_ref[...], preferred_element_type=F32), 0.0)
    acc_scr[...] += jnp.dot((u * u).astype(BF16), wd_ref[...], preferred_element_type=F32)

    @pl.when(j == pl.num_programs(1) - 1)
    def _():
        o_ref[...] = _layer_norm(ALPHA * h_ref[...] + acc_scr[...], g_ref[...], b_ref[...])


def _ffn(h1, w_up, w_down, g, b, n_rows, row0, tm, tf):
    D = h1.shape[1]
    d_ff = w_up.shape[1]
    return pl.pallas_call(
        _ffn_kernel,
        grid=(n_rows // tm, d_ff // tf),
        in_specs=[pl.BlockSpec((pl.Element(tm), pl.Element(D)),
                               lambda i, j: (pl.multiple_of(row0 + i * tm, SUBLANE), 0)),
                  pl.BlockSpec((D, tf), lambda i, j: (0, j)),
                  pl.BlockSpec((tf, D), lambda i, j: (j, 0)),
                  pl.BlockSpec((1, D), lambda i, j: (0, 0)),
                  pl.BlockSpec((1, D), lambda i, j: (0, 0))],
        out_specs=pl.BlockSpec((tm, D), lambda i, j: (i, 0)),
        out_shape=jax.ShapeDtypeStruct((n_rows, D), F32),
        scratch_shapes=[pltpu.VMEM((tm, D), BF16), pltpu.VMEM((tm, D), F32)],
        compiler_params=pltpu.CompilerParams(
            dimension_semantics=("arbitrary", "arbitrary"), vmem_limit_bytes=VMEM_LIMIT),
        name="ffn_ln2",
    )(h1, w_up, w_down, g, b)


def _rope_tables(pos):
    half = ROT_DIM // 2
    inv_freq = jnp.power(jnp.float32(ROPE_THETA), -jnp.arange(0, ROT_DIM, 2, dtype=F32) / ROT_DIM)
    ang = pos.astype(F32)[:, None] * inv_freq[None, :]
    cos, sin = jnp.cos(ang), jnp.sin(ang)
    lane = jnp.arange(D_HEAD) % D_MAP
    f = lane % half
    first, second = lane < half, (lane >= half) & (lane < ROT_DIM)
    cos_t = jnp.where((first | second)[None, :], cos[:, f], 1.0)
    sa_t = jnp.where(first[None, :], -sin[:, f], 0.0)
    sb_t = jnp.where(second[None, :], sin[:, f], 0.0)
    return cos_t, sa_t, sb_t


def kernel(x_prompt, x_sample, cache_fox_k, cache_fox_v, cache_fox_logf, cache_diff_k, cache_diff_v, page_table, meta_tokens, ln_in_g, ln_in_b, w_in, b_forget, lambda_q1, lambda_k1, lambda_q2, lambda_k2, diff_norm_g, w_o, ln1_g, ln1_b, w_up, w_down, ln2_g, ln2_b):
    batch, seq, D = x_prompt.shape
    NS, dec_seq, _ = x_sample.shape
    depth, n_pool, page_size = cache_fox_k.shape[:3]
    n_pages = page_table.shape[1]
    assert batch == 1 and dec_seq == 1 and depth == DEPTH
    assert cache_fox_k.shape[3:] == (N_HEADS, D_HEAD) and meta_tokens.shape[0] == N_META
    past_len = n_pages * page_size

    tm = ROW_TILE
    Lr = N_META + seq
    S0 = _round_up(Lr, 32)
    R = _round_up(S0 + NS, tm)
    rows = _Rows(seq=seq, ns=NS, lr=Lr, s0=S0, r=R, tmh=tm // 2)
    assert rows.n_half >= 2 and R - rows.tmh <= Lr and S0 >= R - rows.tmh and seq >= rows.tmh
    xp, xs, meta = x_prompt[0], x_sample[:, 0], meta_tokens.astype(F32)
    pos = jnp.concatenate([jnp.arange(Lr), jnp.zeros((S0 - Lr,), jnp.int32),
                           jnp.full((NS,), past_len), jnp.zeros((R - S0 - NS,), jnp.int32)])
    cos_t, sa_t, sb_t = _rope_tables(pos)

    w_fox = w_in[0].astype(BF16)
    w_diff = w_fox[:, N_GROUPS // 2 * GROUP + N_HEADS:]
    wf = w_in[0][:, N_GROUPS // 2 * GROUP:N_GROUPS // 2 * GROUP + N_HEADS]
    lane = np.arange(LANE)
    bias_lane = (lane >= BIAS_BASE) & (lane < BIAS_BASE + BIAS_W * N_HEADS)
    head_of_lane = np.where(bias_lane, (lane - BIAS_BASE) // BIAS_W, lane % N_HEADS)
    used = bias_lane | (lane < N_HEADS)
    w_f = jnp.where(used[None, :], wf[:, head_of_lane], 0.0).astype(BF16)
    b_f = jnp.where(used, b_forget[0].astype(F32)[head_of_lane], 0.0)[None, :]
    lane_j = jnp.asarray(np.where(bias_lane, (lane - BIAS_BASE) % BIAS_W, -1)[None, :], jnp.int32)
    row2 = lambda v: v.astype(F32).reshape(1, -1)
    gi, bi = row2(ln_in_g), row2(ln_in_b)

    hb, qb, kb, lf = _ln_gate(xp, meta, xs, gi, bi, w_f, b_f, lane_j, rows)
    zb, p_fk, p_fv, p_dk, p_dv, s_fk, s_fv, s_dk, s_dv = _project(hb, w_fox, w_diff, cos_t, sa_t, sb_t,
                                                                  tm, Lr, S0, NS)

    lamp = jnp.zeros((SUBLANE, LANE), F32).at[0:4, 0:D_MAP].set(
        jnp.stack([lambda_q1[0], lambda_k1[0], lambda_q2[0], lambda_k2[0]]).astype(F32))
    gn = row2(diff_norm_g[0])

    grp = lambda g: zb[S0:S0 + NS, g * GROUP:(g + 1) * GROUP].reshape(NS, N_HEADS, D_HEAD).astype(F32)
    lf_own = jnp.broadcast_to(lf[S0:S0 + NS, :, None], (NS, N_HEADS, LANE))
    page_rows = page_size * N_HEADS
    flat = lambda c: c[0].reshape(n_pool * page_rows, D_HEAD)
    lf_cache_t = jnp.swapaxes(cache_fox_logf[0], 1, 2)
    s_idx = jnp.arange(page_size)
    ux = (s_idx[:, None] > jnp.repeat(s_idx, N_HEADS)[None, :]).astype(BF16)
    qi_tab, ki_tab = _causal_steps(R // tm)
    st = _stream_plan(NS, n_pages, qi_tab.shape[0], page_rows)
    tables = (qi_tab, ki_tab, _step_pages(page_table, st))
    o_f, os_d = _fox_attention(zb, qb, kb, tables, st, tm, (grp(3), grp(4), grp(5)), lamp, gn,
                               flat(cache_diff_k), flat(cache_diff_v))
    o_d, os_f = _diff_attention(zb, lamp, gn, tables, st, tm, (grp(0), grp(1), grp(2), lf_own),
                                flat(cache_fox_k), flat(cache_fox_v), _page_bias(lf_cache_t, ux))

    wo = w_o[0].astype(BF16)
    h1, h1_s = _mix_out(xp, meta, xs, o_f, o_d, os_f.reshape(NS, GROUP).astype(BF16),
                        os_d.reshape(NS, GROUP).astype(BF16), wo[:GROUP], wo[GROUP:], gi, bi,
                        row2(ln1_g[0]), row2(ln1_b[0]), rows)
    ffn = functools.partial(_ffn, w_up=w_up[0].astype(BF16), w_down=w_down[0].astype(BF16),
                            g=row2(ln2_g[0]), b=row2(ln2_b[0]), tf=1024)
    tm_ffn = next(t for t in (512, 256, 128) if seq % t == 0)
    y_prompt = ffn(h1, n_rows=seq, row0=N_META, tm=tm_ffn)
    y_sample = ffn(h1_s, n_rows=NS, row0=0, tm=NS)

    prompt_cache = lambda a: a.reshape(1, 1, Lr, N_HEADS, D_HEAD)
    sample_cache = lambda a: a.reshape(1, NS, 1, N_HEADS, D_HEAD)
    return (y_prompt[None], y_sample[:, None],
            prompt_cache(p_fk), prompt_cache(p_fv), lf[:Lr][None, None], prompt_cache(p_dk), prompt_cache(p_dv),
            sample_cache(s_fk), sample_cache(s_fv), lf[S0:S0 + NS][None, :, None],
            sample_cache(s_dk), sample_cache(s_dv))
```

```python
import functools
import math
from typing import NamedTuple

import jax
import jax.numpy as jnp
import numpy as np
from jax import lax
from jax.experimental import pallas as pl
from jax.experimental.pallas import tpu as pltpu

F32 = jnp.float32
BF16 = jnp.bfloat16

N_META = 16
N_HEADS = 8
D_HEAD = 128
D_MAP = 64
ROT_DIM = D_MAP // 4
ROPE_THETA = 500000.0
DEPTH = 1
ALPHA = (2 * DEPTH) ** 0.25
LN_EPS = 1e-5
NEG_INF = -1e30
LAM_INIT = 0.8 - 0.6 * math.exp(-0.3 * 0)
FOX_SCALE = D_HEAD ** -0.5
DIFF_SCALE = D_MAP ** -0.5
GROUP = N_HEADS * D_HEAD
N_GROUPS = 6
BIAS_BASE = N_HEADS
BIAS_W = 6

LANE = 128
SUBLANE = 8
MXU_DEPTH = 256
VMEM_LIMIT = 58 * 1024 * 1024
HEAD_GROUPS = 2
HEADS_PER_GROUP = N_HEADS // HEAD_GROUPS
GROUP_W = HEADS_PER_GROUP * D_HEAD
DIAG_BLOCKS = 3
ROW_TILE = DIAG_BLOCKS * MXU_DEPTH

NT_DIMS = (((1,), (1,)), ((), ()))


def _round_up(x, m):
    return (x + m - 1) // m * m


def _layer_norm(x, g, b):
    mu = jnp.mean(x, -1, keepdims=True)
    xc = x - mu
    var = jnp.mean(xc * xc, -1, keepdims=True)
    return xc * lax.rsqrt(var + LN_EPS) * g + b


def _split3(x):
    hi = x.astype(BF16).astype(F32)
    r1 = x - hi
    mid = r1.astype(BF16).astype(F32)
    lo = (r1 - mid).astype(BF16).astype(F32)
    return hi, mid, lo


def _rep(x, n):
    return jnp.concatenate([x] * n, axis=1)


class _Rows(NamedTuple):
    seq: int
    ns: int
    lr: int
    s0: int
    r: int
    tmh: int

    @property
    def n_half(self):
        return self.r // self.tmh


def _x_block_spec(rows, d):
    def index(i, *_):
        return pl.multiple_of(jnp.clip(i * rows.tmh - N_META, 0, rows.seq - rows.tmh), SUBLANE), 0
    return pl.BlockSpec((pl.Element(rows.tmh), pl.Element(d)), index)


def _row_space_x(case, blk, meta_ref, xs_ref, rows):
    if case == "first":
        return jnp.concatenate([meta_ref[...], blk[0:rows.tmh - N_META]], axis=0)
    if case == "mid":
        return blk
    n_prompt = rows.lr - (rows.r - rows.tmh)
    parts = [blk[rows.tmh - n_prompt:rows.tmh]]
    for n, val in ((rows.s0 - rows.lr, None), (rows.ns, xs_ref), (rows.r - rows.s0 - rows.ns, None)):
        if n:
            parts.append(jnp.zeros((n, blk.shape[1]), F32) if val is None else val[...])
    return jnp.concatenate(parts, axis=0)


def _per_tile_case(i, n, body):
    pl.when(i == 0)(functools.partial(body, "first"))
    pl.when((i > 0) & (i < n - 1))(functools.partial(body, "mid"))
    pl.when(i == n - 1)(functools.partial(body, "last"))


def _gate_kernel(x_ref, meta_ref, xs_ref, g_ref, b_ref, wf_ref, bf_ref, lanej_ref,
                 hb_ref, qb_ref, kb_ref, lf_ref, carry_scr, *, rows):
    i = pl.program_id(0)
    tm = rows.tmh

    @pl.when(i == 0)
    def _():
        carry_scr[...] = jnp.zeros_like(carry_scr)

    def body(case):
        x = _row_space_x(case, x_ref[...], meta_ref, xs_ref, rows)
        hb = _layer_norm(x, g_ref[...], b_ref[...]).astype(BF16)
        hb_ref[...] = hb
        ff = jnp.dot(hb, wf_ref[...], preferred_element_type=F32) + bf_ref[...]
        lf = jnp.minimum(ff, 0.0) - jnp.log1p(jnp.exp(-jnp.abs(ff)))
        lf_ref[...] = lf[:, 0:N_HEADS]
        row = lax.broadcasted_iota(jnp.int32, (tm, tm), 0)
        col = lax.broadcasted_iota(jnp.int32, (tm, tm), 1)
        tri = (col <= row).astype(F32).astype(BF16)
        pieces = jnp.concatenate(_split3(lf), axis=1).astype(BF16)
        c3 = jnp.dot(tri, pieces, preferred_element_type=F32)
        cs = c3[:, 0:LANE] + c3[:, LANE:2 * LANE] + c3[:, 2 * LANE:3 * LANE] + carry_scr[0:1, :]
        carry_scr[...] = jnp.broadcast_to(cs[tm - 1:tm, :], carry_scr.shape)
        hi, mid, lo = _split3(cs)
        j = jnp.broadcast_to(lanej_ref[...], (tm, LANE))
        one = jnp.ones((tm, LANE), F32)
        zero = jnp.zeros((tm, LANE), F32)
        qb_ref[...] = jnp.where(j == 0, hi, jnp.where(j == 1, mid, jnp.where(
            j == 2, lo, jnp.where(j >= 3, one, zero)))).astype(BF16)
        kb_ref[...] = jnp.where(j == 3, -hi, jnp.where(j == 4, -mid, jnp.where(
            j == 5, -lo, jnp.where(j >= 0, one, zero)))).astype(BF16)

    _per_tile_case(i, rows.n_half, body)


def _ln_gate(x_prompt, meta, x_sample, ln_g, ln_b, w_f, b_f, lane_j, rows):
    D = x_prompt.shape[1]
    tm = rows.tmh
    row = lambda i: (i, 0)
    const = lambda i: (0, 0)
    return pl.pallas_call(
        functools.partial(_gate_kernel, rows=rows),
        grid=(rows.n_half,),
        in_specs=[_x_block_spec(rows, D), pl.BlockSpec(meta.shape, const), pl.BlockSpec(x_sample.shape, const),
                  pl.BlockSpec((1, D), const), pl.BlockSpec((1, D), const),
                  pl.BlockSpec((D, LANE), const), pl.BlockSpec((1, LANE), const), pl.BlockSpec((1, LANE), const)],
        out_specs=[pl.BlockSpec((tm, D), row), pl.BlockSpec((tm, LANE), row),
                   pl.BlockSpec((tm, LANE), row), pl.BlockSpec((tm, N_HEADS), row)],
        out_shape=[jax.ShapeDtypeStruct((rows.r, D), BF16),
                   jax.ShapeDtypeStruct((rows.r, LANE), BF16),
                   jax.ShapeDtypeStruct((rows.r, LANE), BF16),
                   jax.ShapeDtypeStruct((rows.r, N_HEADS), F32)],
        scratch_shapes=[pltpu.VMEM((SUBLANE, LANE), F32)],
        compiler_params=pltpu.CompilerParams(dimension_semantics=("arbitrary",), vmem_limit_bytes=VMEM_LIMIT),
        name="ln_gate",
    )(x_prompt, meta, x_sample, ln_g, ln_b, w_f, b_f, lane_j)


def _rope(zh, cos, sa, sb):
    return zh * cos + pltpu.roll(zh, D_HEAD - ROT_DIM // 2, 1) * sa + pltpu.roll(zh, ROT_DIM // 2, 1) * sb


def _proj_kernel(hb_ref, wfox_ref, wdiff_ref, cos_ref, sa_ref, sb_ref, zb_ref, pk_ref, pv_ref, pdk_ref, pdv_ref,
                 sk_ref, sv_ref, sdk_ref, sdv_ref, *, tm, s_tile, s_off, ns):
    j = pl.program_id(0)
    i = pl.program_id(1)
    heads = [slice(h * D_HEAD, (h + 1) * D_HEAD) for h in range(N_HEADS)]

    def matmul(w_ref):
        return jnp.dot(hb_ref[...], w_ref[...], preferred_element_type=F32)

    def cache_store(per_head, p_ref, s_ref):
        for h in range(N_HEADS):
            p_ref[pl.ds(h, tm, stride=N_HEADS), :] = per_head[h]

        @pl.when(i == s_tile)
        def _():
            for h in range(N_HEADS):
                s_ref[pl.ds(h, ns, stride=N_HEADS), :] = per_head[h][s_off:s_off + ns]

    @pl.when(j == 0)
    def _():
        zb_ref[...] = (matmul(wfox_ref) * FOX_SCALE).astype(BF16)

    def plain(w_ref, p_ref, s_ref):
        z = matmul(w_ref)
        zb_ref[...] = z.astype(BF16)
        cache_store([z[:, hs] for hs in heads], p_ref, s_ref)

    pl.when(j == 1)(functools.partial(plain, wfox_ref, pk_ref, sk_ref))
    pl.when(j == 2)(functools.partial(plain, wfox_ref, pv_ref, sv_ref))
    pl.when(j == 5)(functools.partial(plain, wdiff_ref, pdv_ref, sdv_ref))

    @pl.when(j == 3)
    def _():
        z = matmul(wdiff_ref)
        cos, sa, sb = cos_ref[...], sa_ref[...], sb_ref[...]
        for hs in heads:
            zb_ref[:, hs] = (_rope(z[:, hs], cos, sa, sb) * DIFF_SCALE).astype(BF16)

    @pl.when(j == 4)
    def _():
        z = matmul(wdiff_ref)
        cos, sa, sb = cos_ref[...], sa_ref[...], sb_ref[...]
        ys = [_rope(z[:, hs], cos, sa, sb) for hs in heads]
        for hs, y in zip(heads, ys):
            zb_ref[:, hs] = y.astype(BF16)
        cache_store(ys, pdk_ref, sdk_ref)


def _project(hb, w_fox, w_diff, cos_t, sa_t, sb_t, tm, n_prompt_rows, s0, ns):
    R, D = hb.shape
    n = R // tm
    half = N_GROUPS // 2
    row = lambda j, i: (i, 0)

    def cache_spec(group):
        return pl.BlockSpec((tm * N_HEADS, D_HEAD),
                            lambda j, i: (jnp.where(j < group, 0, jnp.where(j == group, i, n - 1)), 0))

    sample_spec = pl.BlockSpec((ns * N_HEADS, D_HEAD), lambda j, i: (0, 0))
    cache_shape = jax.ShapeDtypeStruct((n_prompt_rows * N_HEADS, D_HEAD), F32)
    sample_shape = jax.ShapeDtypeStruct((ns * N_HEADS, D_HEAD), F32)
    return pl.pallas_call(
        functools.partial(_proj_kernel, tm=tm, s_tile=s0 // tm, s_off=s0 % tm, ns=ns),
        grid=(N_GROUPS, n),
        in_specs=[pl.BlockSpec((tm, D), row),
                  pl.BlockSpec((D, GROUP), lambda j, i: (0, jnp.minimum(j, half - 1))),
                  pl.BlockSpec((D, GROUP), lambda j, i: (0, jnp.maximum(j - half, 0))),
                  pl.BlockSpec((tm, LANE), row), pl.BlockSpec((tm, LANE), row), pl.BlockSpec((tm, LANE), row)],
        out_specs=[pl.BlockSpec((tm, GROUP), lambda j, i: (i, j)),
                   cache_spec(1), cache_spec(2), cache_spec(4), cache_spec(5),
                   sample_spec, sample_spec, sample_spec, sample_spec],
        out_shape=[jax.ShapeDtypeStruct((R, N_GROUPS * GROUP), BF16),
                   cache_shape, cache_shape, cache_shape, cache_shape,
                   sample_shape, sample_shape, sample_shape, sample_shape],
        compiler_params=pltpu.CompilerParams(
            dimension_semantics=("arbitrary", "arbitrary"), vmem_limit_bytes=VMEM_LIMIT),
        name="in_proj",
    )(hb, w_fox, w_diff, cos_t, sa_t, sb_t)


def _flash_block(qs_scr, k_ref, kb_ref, v_ref, m_scr, acc_scr, h, r0, nr, nc, keep):
    hs = slice(h * D_HEAD, (h + 1) * D_HEAD)
    kh = k_ref[0:nc, hs]
    if kb_ref is not None:
        kh = jnp.concatenate([kh, kb_ref[0:nc, :]], axis=1)
    s = lax.dot_general(qs_scr[h, r0:r0 + nr, :], kh, NT_DIMS, preferred_element_type=F32)
    if keep is not None:
        s = jnp.where(keep, s, NEG_INF)
    m_prev = m_scr[h, r0:r0 + nr, :]
    m_new = jnp.maximum(m_prev, jnp.max(s, axis=1, keepdims=True))
    p = jnp.exp(s - _rep(m_new, nc // LANE))
    alpha = jnp.exp(m_prev - m_new)
    vh = jnp.concatenate([v_ref[0:nc, hs], jnp.ones((nc, D_HEAD), BF16)], axis=1)
    pv = jnp.dot(p.astype(BF16), vh, preferred_element_type=F32)
    acc_scr[h, r0:r0 + nr, :] = acc_scr[h, r0:r0 + nr, :] * _rep(alpha, 2) + pv
    m_scr[h, r0:r0 + nr, :] = m_new


def _flash_full(qs_scr, k_ref, kb_ref, v_ref, m_scr, acc_scr, *, tq, nmap):
    for h in range(HEADS_PER_GROUP):
        _flash_block(qs_scr, k_ref, kb_ref, v_ref, m_scr, acc_scr, h, 0, nmap * tq, tq, None)


def _flash_diag(qs_scr, k_ref, kb_ref, v_ref, m_scr, acc_scr, *, tq, nmap):
    sub = tq // DIAG_BLOCKS
    for h in range(HEADS_PER_GROUP):
        for rb in range(DIAG_BLOCKS):
            nc = (rb + 1) * sub
            row = lax.broadcasted_iota(jnp.int32, (sub, nc), 0) + rb * sub
            keep = lax.broadcasted_iota(jnp.int32, (sub, nc), 1) <= row
            for mp in range(nmap):
                _flash_block(qs_scr, k_ref, kb_ref, v_ref, m_scr, acc_scr, h, mp * tq + rb * sub, sub, nc, keep)


def _flash_init(m_scr, acc_scr):
    m_scr[...] = jnp.full(m_scr.shape, NEG_INF, F32)
    acc_scr[...] = jnp.zeros(acc_scr.shape, F32)


def _fox_kernel(qi_tab, ki_tab, pt_ref, q_ref, k_ref, v_ref, qb_ref, kb_ref,
                fq_ref, fk_ref, fv_ref, lfo_ref, *rest, tq, st):
    n = st.pps
    k_pages, v_pages, l_pages = rest[:n], rest[n:2 * n], rest[2 * n:3 * n]
    o_ref, os_ref, qs_scr, m_scr, acc_scr, q16, dm, dl, da, dc = rest[3 * n:]
    g = pl.program_id(0)
    s = pl.program_id(1)
    qi = qi_tab[s]
    ki = ki_tab[s]
    _, _, c, active = _stream_pos(g, s, st)

    @pl.when(active & (c == 0))
    def _():
        _dec_init_fox(fq_ref, fk_ref, fv_ref, lfo_ref, q16, dm, dl, da, dc)

    @pl.when(ki == 0)
    def _():
        _flash_init(m_scr, acc_scr)
        qb = qb_ref[...].astype(F32)
        lane = lax.broadcasted_iota(jnp.int32, qb.shape, 1)
        for h in range(HEADS_PER_GROUP):
            lo = BIAS_BASE + BIAS_W * (g * HEADS_PER_GROUP + h)
            qbh = jnp.where((lane >= lo) & (lane < lo + BIAS_W), qb, 0.0).astype(BF16)
            qs_scr[h] = jnp.concatenate([q_ref[:, h * D_HEAD:(h + 1) * D_HEAD], qbh], axis=1)

    args = (qs_scr, k_ref, kb_ref, v_ref, m_scr, acc_scr)
    decode = functools.partial(_dec_update_fox, q16, k_pages, v_pages, l_pages, dm, dl, da, dc, active)

    @pl.when(ki != qi)
    def _():
        decode()
        _flash_full(*args, tq=tq, nmap=1)

    @pl.when(ki == qi)
    def _():
        decode()
        _flash_diag(*args, tq=tq, nmap=1)
        for h in range(HEADS_PER_GROUP):
            a = acc_scr[h]
            o_ref[:, h * D_HEAD:(h + 1) * D_HEAD] = (a[:, :D_HEAD] / a[:, D_HEAD:]).astype(o_ref.dtype)

    @pl.when(active & (c == st.cps - 1))
    def _():
        os_ref[0] = (da[...] / dl[...])[0:N_HEADS]


def _diff_lambda(lamp_ref):
    lp = lamp_ref[...]
    return (jnp.exp(jnp.sum(lp[0:1] * lp[1:2], axis=1, keepdims=True))
            - jnp.exp(jnp.sum(lp[2:3] * lp[3:4], axis=1, keepdims=True)) + LAM_INIT)


def _diff_finish(o1, o2, lam, gn):
    od = o1 - lam * o2
    od = od * lax.rsqrt(jnp.mean(od * od, axis=-1, keepdims=True) + LN_EPS)
    return od * gn * (1.0 - LAM_INIT)


def _diff_kernel(qi_tab, ki_tab, pt_ref, q_ref, k_ref, v_ref, lamp_ref, gn_ref,
                 dq_ref, dk_ref, dv_ref, *rest, tq, st):
    n = st.pps
    k_pages, v_pages = rest[:n], rest[n:2 * n]
    o_ref, os_ref, qs_scr, m_scr, acc_scr, q16, dm, dl, da = rest[2 * n:]
    g = pl.program_id(0)
    s = pl.program_id(1)
    qi = qi_tab[s]
    ki = ki_tab[s]
    _, _, c, active = _stream_pos(g, s, st)

    @pl.when(active & (c == 0))
    def _():
        _dec_init_diff(dq_ref, dk_ref, dv_ref, q16, dm, dl, da)

    @pl.when(ki == 0)
    def _():
        _flash_init(m_scr, acc_scr)
        lane = lax.broadcasted_iota(jnp.int32, (tq, D_HEAD), 1)
        for h in range(HEADS_PER_GROUP):
            qh = q_ref[:, h * D_HEAD:(h + 1) * D_HEAD].astype(F32)
            qs_scr[h, 0:tq, :] = jnp.where(lane < D_MAP, qh, 0.0).astype(BF16)
            qs_scr[h, tq:2 * tq, :] = jnp.where(lane >= D_MAP, qh, 0.0).astype(BF16)

    args = (qs_scr, k_ref, None, v_ref, m_scr, acc_scr)
    decode = functools.partial(_dec_update_diff, q16, k_pages, v_pages, dm, dl, da, active)

    @pl.when(ki != qi)
    def _():
        decode()
        _flash_full(*args, tq=tq, nmap=2)

    @pl.when(ki == qi)
    def _():
        decode()
        _flash_diag(*args, tq=tq, nmap=2)
        lam = _diff_lambda(lamp_ref)
        gn = gn_ref[...]
        for h in range(HEADS_PER_GROUP):
            a = acc_scr[h]
            o1 = a[0:tq, :D_HEAD] / a[0:tq, D_HEAD:]
            o2 = a[tq:2 * tq, :D_HEAD] / a[tq:2 * tq, D_HEAD:]
            o_ref[:, h * D_HEAD:(h + 1) * D_HEAD] = _diff_finish(o1, o2, lam, gn).astype(o_ref.dtype)

    @pl.when(active & (c == st.cps - 1))
    def _():
        o = da[...] / dl[...]
        os_ref[0] = _diff_finish(o[0:N_HEADS], o[N_HEADS:], _diff_lambda(lamp_ref), gn_ref[...])


def _causal_steps(n_tiles):
    qi = [q for q in range(n_tiles) for _ in range(q + 1)]
    ki = [k for q in range(n_tiles) for k in range(q + 1)]
    return jnp.asarray(qi, jnp.int32), jnp.asarray(ki, jnp.int32)


def _flash_specs(first_group, tq):
    def spec(grp, by_key_tile):
        return pl.BlockSpec((tq, GROUP_W),
                            lambda g, s, qt, kt, pt: ((kt if by_key_tile else qt)[s], grp * HEAD_GROUPS + g))
    return [spec(first_group, False), spec(first_group + 1, True), spec(first_group + 2, True)]


def _attention_call(kernel_fn, name, st, tq, R, tables, flash_in, flash_specs, seq_in, const_in, const_specs,
                    page_arrays, n_maps, q_width, n_state):
    qi_tab, ki_tab, page_table = tables
    per_seq = pl.BlockSpec((1, N_HEADS, D_HEAD), lambda g, s, qt, kt, pt: (_stream_pos(g, s, st)[1], 0, 0))
    page_specs = []
    for arr in page_arrays:
        page_specs += [_page_spec(arr, r, st) for r in range(st.pps)]
    page_args = [arr for arr in page_arrays for _ in range(st.pps)]
    grid_spec = pltpu.PrefetchScalarGridSpec(
        num_scalar_prefetch=3,
        grid=(HEAD_GROUPS, qi_tab.shape[0]),
        in_specs=flash_specs + [per_seq] * len(seq_in) + const_specs + page_specs,
        out_specs=[pl.BlockSpec((tq, GROUP_W), lambda g, s, qt, kt, pt: (qt[s], g)), per_seq],
        scratch_shapes=(
            [pltpu.VMEM((HEADS_PER_GROUP, n_maps * tq, q_width), BF16),
             pltpu.VMEM((HEADS_PER_GROUP, n_maps * tq, LANE), F32),
             pltpu.VMEM((HEADS_PER_GROUP, n_maps * tq, 2 * D_HEAD), F32),
             pltpu.VMEM((2 * N_HEADS, D_HEAD), BF16)]
            + [pltpu.VMEM((2 * N_HEADS, LANE), F32)] * n_state),
    )
    return pl.pallas_call(
        functools.partial(kernel_fn, tq=tq, st=st),
        grid_spec=grid_spec,
        out_shape=[jax.ShapeDtypeStruct((R, GROUP), BF16),
                   jax.ShapeDtypeStruct((st.ns, N_HEADS, D_HEAD), F32)],
        compiler_params=pltpu.CompilerParams(
            dimension_semantics=("arbitrary", "arbitrary"), vmem_limit_bytes=VMEM_LIMIT),
        name=name,
    )(qi_tab, ki_tab, page_table, *flash_in, *seq_in, *const_in, *page_args)


def _fox_attention(zb, qb, kb, tables, st, tq, dec_seq, k_cache, v_cache, bias_cache):
    flash_specs = _flash_specs(0, tq) + [pl.BlockSpec((tq, LANE), lambda g, s, qt, kt, pt: (qt[s], 0)),
                                         pl.BlockSpec((tq, LANE), lambda g, s, qt, kt, pt: (kt[s], 0))]
    return _attention_call(_fox_kernel, "fox_flash", st, tq, zb.shape[0], tables,
                           [zb, zb, zb, qb, kb], flash_specs, dec_seq, [], [],
                           [k_cache, v_cache, bias_cache], n_maps=1, q_width=2 * D_HEAD, n_state=4)


def _diff_attention(zb, lamp, gn, tables, st, tq, dec_seq, dk_cache, dv_cache):
    const = lambda g, s, qt, kt, pt: (0, 0)
    flash_specs = _flash_specs(N_GROUPS // 2, tq) + [pl.BlockSpec((SUBLANE, LANE), const),
                                                     pl.BlockSpec((1, LANE), const)]
    return _attention_call(_diff_kernel, "diff_flash", st, tq, zb.shape[0], tables,
                           [zb, zb, zb, lamp, gn], flash_specs, dec_seq, [], [],
                           [dk_cache, dv_cache], n_maps=2, q_width=D_HEAD, n_state=3)


class _Stream(NamedTuple):
    ns: int
    n_pages: int
    cps: int
    pps: int
    n_pairs: int
    page_rows: int


def _stream_plan(ns, n_pages, n_pairs, page_rows):
    n_steps = HEAD_GROUPS * n_pairs
    fits = [c for c in (8, 4, 2, 1) if ns * c <= n_steps and n_pages % c == 0]
    assert fits, "the attention grid has fewer steps than decode sequences"
    return _Stream(ns=ns, n_pages=n_pages, cps=fits[0], pps=n_pages // fits[0], n_pairs=n_pairs,
                   page_rows=page_rows)


def _step_pages(page_table, st):
    t = np.arange(HEAD_GROUPS * st.n_pairs)
    b = np.minimum(t // st.cps, st.ns - 1)
    cols = st.n_pages - 1 - ((t % st.cps)[:, None] * st.pps + np.arange(st.pps)[None, :])
    return page_table[b[:, None], cols].reshape(-1)


def _stream_pos(g, s, st):
    t = g * st.n_pairs + s
    shift = st.cps.bit_length() - 1
    return t, jnp.minimum(t >> shift, st.ns - 1), t & (st.cps - 1), t < st.ns * st.cps


def _page_spec(cache, r, st):
    def page(g, s, qt, kt, pt):
        return pt[(g * st.n_pairs + s) * st.pps + r]
    if cache.shape[1] != D_HEAD:
        return pl.BlockSpec((N_HEADS, cache.shape[1]), lambda *a: (page(*a), 0))
    return pl.BlockSpec((st.page_rows, D_HEAD), lambda *a: (page(*a), 0))


def _page_bias_kernel(l_ref, ux_ref, o_ref):
    lt = l_ref[...]
    n = lt.shape[0] * N_HEADS
    lt = lt.reshape(n, lt.shape[2])
    b3 = jnp.dot(jnp.concatenate(_split3(lt), axis=0).astype(BF16), ux_ref[...], preferred_element_type=F32)
    flat = ux_ref.shape[1]
    o_ref[:, 0:flat] = b3[0:n] + b3[n:2 * n] + b3[2 * n:3 * n]
    o_ref[:, flat:] = lt


def _page_bias(lf_cache_t, ux):
    n_pool, _, page = lf_cache_t.shape
    flat = ux.shape[1]
    g = next(c for c in (64, 32, 16, 8, 4, 2, 1) if n_pool % c == 0)
    return pl.pallas_call(
        _page_bias_kernel,
        grid=(n_pool // g,),
        in_specs=[pl.BlockSpec((g, N_HEADS, page), lambda i: (i, 0, 0)), pl.BlockSpec(ux.shape, lambda i: (0, 0))],
        out_specs=pl.BlockSpec((g * N_HEADS, flat + page), lambda i: (i, 0)),
        out_shape=jax.ShapeDtypeStruct((n_pool * N_HEADS, flat + page), F32),
        compiler_params=pltpu.CompilerParams(dimension_semantics=("arbitrary",), vmem_limit_bytes=VMEM_LIMIT),
        name="page_bias",
    )(lf_cache_t, ux)


def _dup(x):
    return jnp.concatenate([x, x], axis=0)


def _dec_init_fox(q_ref, k_ref, v_ref, lfo_ref, q16, m, l, a, c):
    q2 = _dup(q_ref[0])
    q16[...] = q2.astype(BF16)
    m[...] = jnp.broadcast_to(jnp.sum(q2 * _dup(k_ref[0]), axis=1, keepdims=True), m.shape)
    l[...] = jnp.ones(l.shape, F32)
    a[...] = _dup(v_ref[0])
    c[...] = _dup(lfo_ref[0])


def _dec_init_diff(q_ref, k_ref, v_ref, q16, m, l, a):
    lane = lax.broadcasted_iota(jnp.int32, (N_HEADS, D_HEAD), 1)
    qd = q_ref[0]
    q2 = jnp.concatenate([jnp.where(lane < D_MAP, qd, 0.0), jnp.where(lane >= D_MAP, qd, 0.0)], axis=0)
    q16[...] = q2.astype(BF16)
    m[...] = jnp.broadcast_to(jnp.sum(q2 * _dup(k_ref[0]), axis=1, keepdims=True), m.shape)
    l[...] = jnp.ones(l.shape, F32)
    a[...] = _dup(v_ref[0])


def _head_match(page_rows):
    lane8 = lax.broadcasted_iota(jnp.int32, (2 * N_HEADS, page_rows), 1) & (N_HEADS - 1)
    row8 = lax.broadcasted_iota(jnp.int32, (2 * N_HEADS, page_rows), 0) & (N_HEADS - 1)
    return lane8 == row8


def _dec_update_fox(q16, k_pages, v_pages, l_pages, m_ref, l_ref, a_ref, c_ref, active):
    page_rows = k_pages[0].shape[0]
    n_rep = page_rows // LANE
    valid = _head_match(page_rows)
    q = q16[...]
    carry = c_ref[...]
    scores = []
    for k_page, l_page in zip(k_pages, l_pages):
        s_all = lax.dot_general(q, k_page[...].astype(BF16), NT_DIMS, preferred_element_type=F32)
        bias = _dup(l_page[:, 0:page_rows]) + _rep(carry, n_rep)
        scores.append(jnp.where(valid, s_all + bias, NEG_INF))
        carry = carry + _dup(jnp.sum(l_page[:, page_rows:], axis=1, keepdims=True))
    c_ref[...] = jnp.where(active, carry, c_ref[...])
    _dec_online(scores, v_pages, m_ref, l_ref, a_ref, active)


def _dec_update_diff(q16, k_pages, v_pages, m_ref, l_ref, a_ref, active):
    valid = _head_match(k_pages[0].shape[0])
    q = q16[...]
    scores = [jnp.where(valid, lax.dot_general(q, k_page[...].astype(BF16), NT_DIMS, preferred_element_type=F32),
                        NEG_INF) for k_page in k_pages]
    _dec_online(scores, v_pages, m_ref, l_ref, a_ref, active)


def _dec_online(scores, v_pages, m_ref, l_ref, a_ref, active):
    n_rep = scores[0].shape[1] // LANE
    m_prev = m_ref[...]
    m_new = m_prev
    for s in scores:
        m_new = jnp.maximum(m_new, jnp.max(s, axis=1, keepdims=True))
    alpha = jnp.exp(m_prev - m_new)
    m_rep = _rep(m_new, n_rep)
    l_new = alpha * l_ref[...]
    a_new = alpha * a_ref[...]
    for s, v_page in zip(scores, v_pages):
        p = jnp.exp(s - m_rep)
        l_new = l_new + jnp.sum(p, axis=1, keepdims=True)
        a_new = a_new + jnp.dot(p.astype(BF16), v_page[...].astype(BF16), preferred_element_type=F32)
    l_ref[...] = jnp.where(active, l_new, l_ref[...])
    a_ref[...] = jnp.where(active, a_new, a_ref[...])
    m_ref[...] = jnp.where(active, m_new, m_prev)


def _mix_kernel(x_ref, meta_ref, xs_ref, of_ref, od_ref, sf_ref, sd_ref, wt_ref, wb_ref, gi_ref, bi_ref,
                g_ref, b_ref, o_ref, os_ref, *, rows):
    def mixed(x, o_f, o_d):
        y = (ALPHA * _layer_norm(x, gi_ref[...], bi_ref[...])
             + jnp.dot(o_f, wt_ref[...], preferred_element_type=F32)
             + jnp.dot(o_d, wb_ref[...], preferred_element_type=F32))
        return _layer_norm(y, g_ref[...], b_ref[...])

    def body(case):
        o_ref[...] = mixed(_row_space_x(case, x_ref[...], meta_ref, xs_ref, rows), of_ref[...], od_ref[...])
        if case == "last":
            os_ref[...] = mixed(xs_ref[...], sf_ref[...], sd_ref[...])

    _per_tile_case(pl.program_id(0), rows.n_half, body)


def _mix_out(x_prompt, meta, x_sample, o_f, o_d, os_f, os_d, wo_top, wo_bot, gi, bi, g, b, rows):
    D = x_prompt.shape[1]
    tm = rows.tmh
    row = lambda i: (i, 0)
    const = lambda i: (0, 0)
    return pl.pallas_call(
        functools.partial(_mix_kernel, rows=rows),
        grid=(rows.n_half,),
        in_specs=[_x_block_spec(rows, D), pl.BlockSpec(meta.shape, const), pl.BlockSpec(x_sample.shape, const),
                  pl.BlockSpec((tm, GROUP), row), pl.BlockSpec((tm, GROUP), row),
                  pl.BlockSpec((rows.ns, GROUP), const), pl.BlockSpec((rows.ns, GROUP), const),
                  pl.BlockSpec((GROUP, D), const), pl.BlockSpec((GROUP, D), const),
                  pl.BlockSpec((1, D), const), pl.BlockSpec((1, D), const),
                  pl.BlockSpec((1, D), const), pl.BlockSpec((1, D), const)],
        out_specs=[pl.BlockSpec((tm, D), row), pl.BlockSpec((rows.ns, D), const)],
        out_shape=[jax.ShapeDtypeStruct((rows.r, D), F32),
                   jax.ShapeDtypeStruct((rows.ns, D), F32)],
        compiler_params=pltpu.CompilerParams(dimension_semantics=("arbitrary",), vmem_limit_bytes=VMEM_LIMIT),
        name="mix_out_ln1",
    )(x_prompt, meta, x_sample, o_f, o_d, os_f, os_d, wo_top, wo_bot, gi, bi, g, b)


def _ffn_kernel(h_ref, wu_ref, wd_ref, g_ref, b_ref, o_ref, hb_scr, acc_scr):
    j = pl.program_id(1)

    @pl.when(j == 0)
    def _():
        hb_scr[...] = h_ref[...].astype(BF16)
        acc_scr[...] = jnp.zeros_like(acc_scr)

    u = jnp.maximum(jnp.dot(hb_scr[...], wu_ref[...], preferred_element_type=F32), 0.0)
    acc_scr[...] += jnp.dot((u * u).astype(BF16), wd_ref[...], preferred_element_type=F32)

    @pl.when(j == pl.num_programs(1) - 1)
    def _():
        o_ref[...] = _layer_norm(ALPHA * h_ref[...] + acc_scr[...], g_ref[...], b_ref[...])


def _ffn(h1, w_up, w_down, g, b, n_rows, row0, tm, tf):
    D = h1.shape[1]
    d_ff = w_up.shape[1]
    return pl.pallas_call(
        _ffn_kernel,
        grid=(n_rows // tm, d_ff // tf),
        in_specs=[pl.BlockSpec((pl.Element(tm), pl.Element(D)),
                               lambda i, j: (pl.multiple_of(row0 + i * tm, SUBLANE), 0)),
                  pl.BlockSpec((D, tf), lambda i, j: (0, j)),
                  pl.BlockSpec((tf, D), lambda i, j: (j, 0)),
                  pl.BlockSpec((1, D), lambda i, j: (0, 0)),
                  pl.BlockSpec((1, D), lambda i, j: (0, 0))],
        out_specs=pl.BlockSpec((tm, D), lambda i, j: (i, 0)),
        out_shape=jax.ShapeDtypeStruct((n_rows, D), F32),
        scratch_shapes=[pltpu.VMEM((tm, D), BF16), pltpu.VMEM((tm, D), F32)],
        compiler_params=pltpu.CompilerParams(
            dimension_semantics=("arbitrary", "arbitrary"), vmem_limit_bytes=VMEM_LIMIT),
        name="ffn_ln2",
    )(h1, w_up, w_down, g, b)


def _rope_tables(pos):
    half = ROT_DIM // 2
    inv_freq = jnp.power(jnp.float32(ROPE_THETA), -jnp.arange(0, ROT_DIM, 2, dtype=F32) / ROT_DIM)
    ang = pos.astype(F32)[:, None] * inv_freq[None, :]
    cos, sin = jnp.cos(ang), jnp.sin(ang)
    lane = jnp.arange(D_HEAD) % D_MAP
    f = lane % half
    first, second = lane < half, (lane >= half) & (lane < ROT_DIM)
    cos_t = jnp.where((first | second)[None, :], cos[:, f], 1.0)
    sa_t = jnp.where(first[None, :], -sin[:, f], 0.0)
    sb_t = jnp.where(second[None, :], sin[:, f], 0.0)
    return cos_t, sa_t, sb_t


def kernel(x_prompt, x_sample, cache_fox_k, cache_fox_v, cache_fox_logf, cache_diff_k, cache_diff_v, page_table, meta_tokens, ln_in_g, ln_in_b, w_in, b_forget, lambda_q1, lambda_k1, lambda_q2, lambda_k2, diff_norm_g, w_o, ln1_g, ln1_b, w_up, w_down, ln2_g, ln2_b):
    batch, seq, D = x_prompt.shape
    NS, dec_seq, _ = x_sample.shape
    depth, n_pool, page_size = cache_fox_k.shape[:3]
    n_pages = page_table.shape[1]
    assert batch == 1 and dec_seq == 1 and depth == DEPTH
    assert cache_fox_k.shape[3:] == (N_HEADS, D_HEAD) and meta_tokens.shape[0] == N_META
    past_len = n_pages * page_size

    tm = ROW_TILE
    Lr = N_META + seq
    S0 = _round_up(Lr, 32)
    R = _round_up(S0 + NS, tm)
    rows = _Rows(seq=seq, ns=NS, lr=Lr, s0=S0, r=R, tmh=tm // 2)
    assert rows.n_half >= 2 and R - rows.tmh <= Lr and S0 >= R - rows.tmh and seq >= rows.tmh
    xp, xs, meta = x_prompt[0], x_sample[:, 0], meta_tokens.astype(F32)
    pos = jnp.concatenate([jnp.arange(Lr), jnp.zeros((S0 - Lr,), jnp.int32),
                           jnp.full((NS,), past_len), jnp.zeros((R - S0 - NS,), jnp.int32)])
    cos_t, sa_t, sb_t = _rope_tables(pos)

    w_fox = w_in[0].astype(BF16)
    w_diff = w_fox[:, N_GROUPS // 2 * GROUP + N_HEADS:]
    wf = w_in[0][:, N_GROUPS // 2 * GROUP:N_GROUPS // 2 * GROUP + N_HEADS]
    lane = np.arange(LANE)
    bias_lane = (lane >= BIAS_BASE) & (lane < BIAS_BASE + BIAS_W * N_HEADS)
    head_of_lane = np.where(bias_lane, (lane - BIAS_BASE) // BIAS_W, lane % N_HEADS)
    used = bias_lane | (lane < N_HEADS)
    w_f = jnp.where(used[None, :], wf[:, head_of_lane], 0.0).astype(BF16)
    b_f = jnp.where(used, b_forget[0].astype(F32)[head_of_lane], 0.0)[None, :]
    lane_j = jnp.asarray(np.where(bias_lane, (lane - BIAS_BASE) % BIAS_W, -1)[None, :], jnp.int32)
    row2 = lambda v: v.astype(F32).reshape(1, -1)
    gi, bi = row2(ln_in_g), row2(ln_in_b)

    hb, qb, kb, lf = _ln_gate(xp, meta, xs, gi, bi, w_f, b_f, lane_j, rows)
    zb, p_fk, p_fv, p_dk, p_dv, s_fk, s_fv, s_dk, s_dv = _project(hb, w_fox, w_diff, cos_t, sa_t, sb_t,
                                                                  tm, Lr, S0, NS)

    lamp = jnp.zeros((SUBLANE, LANE), F32).at[0:4, 0:D_MAP].set(
        jnp.stack([lambda_q1[0], lambda_k1[0], lambda_q2[0], lambda_k2[0]]).astype(F32))
    gn = row2(diff_norm_g[0])

    grp = lambda g: zb[S0:S0 + NS, g * GROUP:(g + 1) * GROUP].reshape(NS, N_HEADS, D_HEAD).astype(F32)
    lf_own = jnp.broadcast_to(lf[S0:S0 + NS, :, None], (NS, N_HEADS, LANE))
    page_rows = page_size * N_HEADS
    flat = lambda c: c[0].reshape(n_pool * page_rows, D_HEAD)
    lf_cache_t = jnp.swapaxes(cache_fox_logf[0], 1, 2)
    s_idx = jnp.arange(page_size)
    ux = (s_idx[:, None] > jnp.repeat(s_idx, N_HEADS)[None, :]).astype(BF16)
    qi_tab, ki_tab = _causal_steps(R // tm)
    st = _stream_plan(NS, n_pages, qi_tab.shape[0], page_rows)
    tables = (qi_tab, ki_tab, _step_pages(page_table, st))
    o_f, os_f = _fox_attention(zb, qb, kb, tables, st, tm, (grp(0), grp(1), grp(2), lf_own),
                               flat(cache_fox_k), flat(cache_fox_v), _page_bias(lf_cache_t, ux))
    o_d, os_d = _diff_attention(zb, lamp, gn, tables, st, tm, (grp(3), grp(4), grp(5)),
                                flat(cache_diff_k), flat(cache_diff_v))

    wo = w_o[0].astype(BF16)
    h1, h1_s = _mix_out(xp, meta, xs, o_f, o_d, os_f.reshape(NS, GROUP).astype(BF16),
                        os_d.reshape(NS, GROUP).astype(BF16), wo[:GROUP], wo[GROUP:], gi, bi,
                        row2(ln1_g[0]), row2(ln1_b[0]), rows)
    ffn = functools.partial(_ffn, w_up=w_up[0].astype(BF16), w_down=w_down[0].astype(BF16),
                            g=row2(ln2_g[0]), b=row2(ln2_b[0]), tf=1024)
    tm_ffn = next(t for t in (512, 256, 128) if seq % t == 0)
    y_prompt = ffn(h1, n_rows=seq, row0=N_META, tm=tm_ffn)
    y_sample = ffn(h1_s, n_rows=NS, row0=0, tm=NS)

    prompt_cache = lambda a: a.reshape(1, 1, Lr, N_HEADS, D_HEAD)
    sample_cache = lambda a: a.reshape(1, NS, 1, N_HEADS, D_HEAD)
    return (y_prompt[None], y_sample[:, None],
            prompt_cache(p_fk), prompt_cache(p_fv), lf[:Lr][None, None], prompt_cache(p_dk), prompt_cache(p_dv),
            sample_cache(s_fk), sample_cache(s_fv), lf[S0:S0 + NS][None, :, None],
            sample_cache(s_dk), sample_cache(s_dv))
```
